```python
import math
import jax, jax.numpy as jnp
from jax import lax
import numpy as np

D_MODEL = 1024
BATCH = 16
SEQ = 4096
DEPTH = 1

CHUNK = 64
N_LEFT_CHUNKS = 8
BAND = N_LEFT_CHUNKS + 1
ATT_HEADS = 8
HEAD_DIM = 64
ATT_WIDTH = ATT_HEADS * HEAD_DIM
REL_CLIP = 128
SSM_WIDTH = D_MODEL // 2
SSM_GROUP = 16
SSM_GROUPS = SSM_WIDTH // SSM_GROUP
SSM_STATE = 64
N_BRANCH = 2
IN_WIDTH = 3 * ATT_WIDTH + SSM_WIDTH + N_BRANCH * D_MODEL
D_FF = 4 * D_MODEL
EPS = 1e-6
DT_MIN = 1e-3
DT_MAX = 1e-1
NEG_INF = -1e30

kernel_name = 'chunk_causal_attn_s5_gated_hybrid'


def rms_norm(x, g):
    xf = x.astype(jnp.float32)
    y = xf * lax.rsqrt(jnp.mean(xf * xf, axis=-1, keepdims=True) + EPS)
    return (y * g.astype(jnp.float32)).astype(x.dtype)


def chunked_band_attention(q, k, v, rel_bias):
    b, s, h, dh = q.shape
    nc = s // CHUNK
    pad = N_LEFT_CHUNKS * CHUNK
    qc = q.reshape(b, nc, CHUNK, h, dh)
    kp = jnp.pad(k, ((0, 0), (pad, 0), (0, 0), (0, 0))).reshape(b, nc + N_LEFT_CHUNKS, CHUNK, h, dh)
    vp = jnp.pad(v, ((0, 0), (pad, 0), (0, 0), (0, 0))).reshape(b, nc + N_LEFT_CHUNKS, CHUNK, h, dh)
    w = jnp.arange(BAND)
    band_idx = jnp.arange(nc)[:, None] + w[None, :]
    kb = kp[:, band_idx]
    vb = vp[:, band_idx]
    scores = jnp.einsum('bnchd,bnwjhd->bnhcwj', qc, kb).astype(jnp.float32) * (dh ** -0.5)
    c = jnp.arange(CHUNK)
    dist = (BAND - 1 - w)[None, :, None] * CHUNK + c[:, None, None] - c[None, None, :]
    bias = rel_bias.astype(jnp.float32)[:, jnp.clip(dist, -REL_CLIP, REL_CLIP) + REL_CLIP]
    valid = (jnp.arange(nc)[:, None] - N_LEFT_CHUNKS + w[None, :]) >= 0
    scores = jnp.where(valid[None, :, None, None, :, None], scores + bias[None, None], NEG_INF)
    probs = jax.nn.softmax(scores.reshape(b, nc, h, CHUNK, BAND * CHUNK), axis=-1)
    probs = probs.reshape(b, nc, h, CHUNK, BAND, CHUNK).astype(v.dtype)
    out = jnp.einsum('bnhcwj,bnwjhd->bnchd', probs, vb)
    return out.reshape(b, s, h * dh)


def s5_ssm(u, a_re, a_im, log_dt, b_re, b_im, c_re, c_im, d_skip):
    f32 = jnp.float32
    b, s, _ = u.shape
    uf = u.astype(f32).reshape(b, s, SSM_GROUPS, SSM_GROUP)
    lam = lax.complex(a_re.astype(f32), a_im.astype(f32))
    dt = jnp.exp(log_dt.astype(f32))[:, None]
    lam_bar = jnp.exp(lam * dt)
    b_bar = ((lam_bar - 1.0) / lam)[..., None] * lax.complex(b_re.astype(f32), b_im.astype(f32))
    bu = jnp.einsum('gph,bsgh->bsgp', b_bar, uf.astype(jnp.complex64))
    a_seq = jnp.broadcast_to(lam_bar, bu.shape)

    def combine(e1, e2):
        a1, x1 = e1
        a2, x2 = e2
        return a1 * a2, a2 * x1 + x2

    _, states = lax.associative_scan(combine, (a_seq, bu), axis=1)
    c_mat = lax.complex(c_re.astype(f32), c_im.astype(f32))
    y = jnp.einsum('ghp,bsgp->bsgh', c_mat, states).real + d_skip.astype(f32) * uf
    return y.reshape(b, s, SSM_WIDTH).astype(u.dtype)


def mixer_block(h, norm_g, w_in, b_gate, rel_bias, a_re, a_im, log_dt, b_re, b_im, c_re, c_im,
                d_skip, w_glu, w_proj_a, w_proj_b, w_out):
    b, s, _ = h.shape
    u = rms_norm(h, norm_g)
    z = u @ w_in
    q, k, v, us, zg = jnp.split(
        z, [ATT_WIDTH, 2 * ATT_WIDTH, 3 * ATT_WIDTH, 3 * ATT_WIDTH + SSM_WIDTH], axis=-1)
    heads = (b, s, ATT_HEADS, HEAD_DIM)
    att = chunked_band_attention(q.reshape(heads), k.reshape(heads), v.reshape(heads), rel_bias)
    y_a = att @ w_proj_a
    ys = jax.nn.gelu(s5_ssm(us, a_re, a_im, log_dt, b_re, b_im, c_re, c_im, d_skip))
    glu_v, glu_g = jnp.split(ys @ w_glu, 2, axis=-1)
    y_b = (glu_v * jax.nn.sigmoid(glu_g)) @ w_proj_b
    gates = jax.nn.sigmoid(zg + b_gate).reshape(b, s, N_BRANCH, D_MODEL)
    mixed = gates[:, :, 0] * y_a + gates[:, :, 1] * y_b
    return mixed @ w_out


def setup_inputs(seed: int = 0) -> dict:
    key = jax.random.key(seed)
    ks = jax.random.split(key, 24)
    L, G, P, Hg = DEPTH, SSM_GROUPS, SSM_STATE, SSM_GROUP
    nrm = lambda k, shape, scale: jax.random.normal(k, shape, jnp.float32) * scale
    x = jax.random.normal(ks[0], (BATCH, SEQ, D_MODEL), jnp.float32)
    norm_mix = 1.0 + nrm(ks[1], (L, D_MODEL), 0.01)
    w_in = nrm(ks[2], (L, D_MODEL, IN_WIDTH), D_MODEL ** -0.5)
    b_gate = nrm(ks[3], (L, N_BRANCH * D_MODEL), 0.01)
    rel_bias = nrm(ks[4], (L, ATT_HEADS, 2 * REL_CLIP + 1), 0.5)
    ssm_a_re = -0.5 * jnp.exp(nrm(ks[5], (L, G, P), 0.01))
    ssm_a_im = math.pi * jnp.arange(P, dtype=jnp.float32)[None, None, :] * (1.0 + nrm(ks[6], (L, G, P), 0.01))
    ssm_log_dt = jax.random.uniform(ks[7], (L, G), jnp.float32, math.log(DT_MIN), math.log(DT_MAX))
    ssm_b_re = nrm(ks[8], (L, G, P, Hg), (2.0 * Hg) ** -0.5)
    ssm_b_im = nrm(ks[9], (L, G, P, Hg), (2.0 * Hg) ** -0.5)
    ssm_c_re = nrm(ks[10], (L, G, Hg, P), (2.0 * P) ** -0.5)
    ssm_c_im = nrm(ks[11], (L, G, Hg, P), (2.0 * P) ** -0.5)
    ssm_d = nrm(ks[12], (L, G, Hg), 1.0)
    w_glu = nrm(ks[13], (L, SSM_WIDTH, 2 * SSM_WIDTH), SSM_WIDTH ** -0.5)
    w_proj_a = nrm(ks[14], (L, ATT_WIDTH, D_MODEL), ATT_WIDTH ** -0.5)
    w_proj_b = nrm(ks[15], (L, SSM_WIDTH, D_MODEL), SSM_WIDTH ** -0.5)
    w_out = nrm(ks[16], (L, D_MODEL, D_MODEL), D_MODEL ** -0.5)
    norm_ffn = 1.0 + nrm(ks[17], (L, D_MODEL), 0.01)
    w_ff1 = nrm(ks[18], (L, D_MODEL, D_FF), D_MODEL ** -0.5)
    w_ff2 = nrm(ks[19], (L, D_FF, D_MODEL), D_FF ** -0.5)
    norm_final = 1.0 + nrm(ks[20], (D_MODEL,), 0.01)
    return {'x': x, 'norm_mix': norm_mix, 'w_in': w_in, 'b_gate': b_gate, 'rel_bias': rel_bias,
            'ssm_a_re': ssm_a_re, 'ssm_a_im': ssm_a_im, 'ssm_log_dt': ssm_log_dt,
            'ssm_b_re': ssm_b_re, 'ssm_b_im': ssm_b_im, 'ssm_c_re': ssm_c_re, 'ssm_c_im': ssm_c_im,
            'ssm_d': ssm_d, 'w_glu': w_glu, 'w_proj_a': w_proj_a, 'w_proj_b': w_proj_b,
            'w_out': w_out, 'norm_ffn': norm_ffn, 'w_ff1': w_ff1, 'w_ff2': w_ff2,
            'norm_final': norm_final}


def reference(x, norm_mix, w_in, b_gate, rel_bias, ssm_a_re, ssm_a_im, ssm_log_dt,
              ssm_b_re, ssm_b_im, ssm_c_re, ssm_c_im, ssm_d, w_glu, w_proj_a, w_proj_b,
              w_out, norm_ffn, w_ff1, w_ff2, norm_final):
    h = x
    for l in range(DEPTH):
        h = h + mixer_block(h, norm_mix[l], w_in[l], b_gate[l], rel_bias[l], ssm_a_re[l], ssm_a_im[l],
                            ssm_log_dt[l], ssm_b_re[l], ssm_b_im[l], ssm_c_re[l], ssm_c_im[l],
                            ssm_d[l], w_glu[l], w_proj_a[l], w_proj_b[l], w_out[l])
        f = rms_norm(h, norm_ffn[l]) @ w_ff1[l]
        h = h + jnp.square(jax.nn.relu(f)) @ w_ff2[l]
    return rms_norm(h, norm_final)
```

```python
import functools
import math

import jax
import jax.numpy as jnp
from jax import lax
from jax.experimental import pallas as pl
from jax.experimental.pallas import tpu as pltpu

F32 = jnp.float32
BF16 = jnp.bfloat16

CHUNK = 64
N_LEFT_CHUNKS = 8
ATT_HEADS = 8
HEAD_DIM = 64
ATT_WIDTH = ATT_HEADS * HEAD_DIM
REL_CLIP = 128
SSM_GROUP = 16
SSM_STATE = 64
EPS = 1e-6
NEG_INF = -1e30

LANES = 128
Q_BLOCK = 2 * CHUNK
KEY_WINDOW = Q_BLOCK + N_LEFT_CHUNKS * CHUNK
LEFT_BLOCKS = (N_LEFT_CHUNKS * CHUNK) // Q_BLOCK
SSM_T = 16
SSM_PAIR = 2
VMEM_LIMIT = 56 * 1024 * 1024


def _const_spec(shape):
    nd = len(shape)
    return pl.BlockSpec(shape, lambda *_: (0,) * nd, pipeline_mode=pl.Buffered(1))


def _rms(x, g):
    ms = jnp.mean(x * x, axis=-1, keepdims=True)
    return x * lax.rsqrt(ms + EPS) * g


def _in_proj_kernel(x_ref, g_ref, w_ref, bg_ref, q_ref, k_ref, v_ref, us_ref, gate_ref):
    u = _rms(x_ref[...], g_ref[...]).astype(BF16)

    def proj(lo, hi):
        return jnp.dot(u, w_ref[:, lo:hi], preferred_element_type=F32)

    aw = ATT_WIDTH
    q_ref[...] = (proj(0, aw) * (HEAD_DIM ** -0.5)).astype(BF16)
    k_ref[...] = proj(aw, 2 * aw).astype(BF16)
    v_ref[...] = proj(2 * aw, 3 * aw).astype(BF16)
    sw = us_ref.shape[1]
    us_ref[...] = proj(3 * aw, 3 * aw + sw).astype(BF16)
    g0 = 3 * aw + sw
    gw = gate_ref.shape[1]
    half = gw // 2
    for lo in (0, half):
        z = proj(g0 + lo, g0 + lo + half) + bg_ref[:, lo:lo + half]
        gate_ref[:, lo:lo + half] = jax.nn.sigmoid(z).astype(BF16)


def _in_proj(x2, norm_g, w_in, b_gate, tm):
    m, d = x2.shape
    n_in = w_in.shape[1]
    gw = b_gate.shape[0]
    sw = n_in - 3 * ATT_WIDTH - gw
    row = lambda w: pl.BlockSpec((tm, w), lambda i: (i, 0))
    return pl.pallas_call(
        _in_proj_kernel,
        grid=(m // tm,),
        in_specs=[row(d), _const_spec((1, d)), _const_spec((d, n_in)), _const_spec((1, gw))],
        out_specs=[row(ATT_WIDTH), row(ATT_WIDTH), row(ATT_WIDTH), row(sw), row(gw)],
        out_shape=[jax.ShapeDtypeStruct((m, ATT_WIDTH), BF16)] * 3
        + [jax.ShapeDtypeStruct((m, sw), BF16), jax.ShapeDtypeStruct((m, gw), BF16)],
        compiler_params=pltpu.CompilerParams(
            dimension_semantics=("arbitrary",), vmem_limit_bytes=VMEM_LIMIT),
        name="in_proj",
    )(x2, norm_g.reshape(1, d), w_in.astype(BF16), b_gate.reshape(1, gw))


def _attn_kernel(q_ref, k_ref, v_ref, bias_ref, o_ref):
    i = pl.program_id(1)
    start = pl.multiple_of(jnp.maximum(i - LEFT_BLOCKS, 0) * Q_BLOCK, Q_BLOCK)
    first_half = lax.broadcasted_iota(jnp.int32, (Q_BLOCK, LANES), 1) < HEAD_DIM
    for p in range(ATT_HEADS // 2):
        cols = slice(p * LANES, (p + 1) * LANES)
        qp = q_ref[0, :, cols]
        kp = k_ref[0, pl.ds(start, KEY_WINDOW), cols]
        vp = v_ref[0, pl.ds(start, KEY_WINDOW), cols]
        outs = []
        for e in range(2):
            qm = jnp.where(first_half if e == 0 else jnp.logical_not(first_half), qp, jnp.zeros_like(qp))
            s = lax.dot_general(qm, kp, (((1,), (1,)), ((), ())), preferred_element_type=F32)
            s = s + bias_ref[0, 2 * p + e]
            mx = jnp.max(s, axis=-1, keepdims=True)
            pe = jnp.exp(s - mx)
            den = jnp.sum(pe, axis=-1, keepdims=True)
            o = jnp.dot(pe.astype(BF16), vp, preferred_element_type=F32)
            outs.append(o / den)
        o_ref[0, :, cols] = jnp.where(first_half, outs[0], outs[1]).astype(BF16)


def _attn_bias_tables(rel_bias):
    t = jnp.arange(LEFT_BLOCKS + 1)[:, None, None]
    qr = jnp.arange(Q_BLOCK)[None, :, None]
    r = jnp.arange(KEY_WINDOW)[None, None, :]
    qpos = t * Q_BLOCK + qr
    kpos = jnp.maximum(t - LEFT_BLOCKS, 0) * Q_BLOCK + r
    dchunk = qpos // CHUNK - kpos // CHUNK
    valid = (dchunk >= 0) & (dchunk <= N_LEFT_CHUNKS)
    idx = jnp.clip(qpos - kpos, -REL_CLIP, REL_CLIP) + REL_CLIP
    bias = rel_bias.astype(F32)[:, idx]
    bias = jnp.where(valid[None], bias, NEG_INF)
    return jnp.transpose(bias, (1, 0, 2, 3))


def _band_attn(q, k, v, rel_bias):
    b, s, w = q.shape
    nq = s // Q_BLOCK
    bias = _attn_bias_tables(rel_bias)
    whole = pl.BlockSpec((1, s, w), lambda bi, i: (bi, 0, 0))
    return pl.pallas_call(
        _attn_kernel,
        grid=(b, nq),
        in_specs=[
            pl.BlockSpec((1, Q_BLOCK, w), lambda bi, i: (bi, i, 0)),
            whole, whole,
            pl.BlockSpec((1, ATT_HEADS, Q_BLOCK, KEY_WINDOW),
                         lambda bi, i: (jnp.minimum(i, LEFT_BLOCKS), 0, 0, 0)),
        ],
        out_specs=pl.BlockSpec((1, Q_BLOCK, w), lambda bi, i: (bi, i, 0)),
        out_shape=jax.ShapeDtypeStruct((b, s, w), BF16),
        compiler_params=pltpu.CompilerParams(
            dimension_semantics=("arbitrary", "arbitrary"), vmem_limit_bytes=VMEM_LIMIT),
        name="band_attn",
    )(q, k, v, bias)


def _ssm_kernel(u_ref, min_ref, toep_ref, mout_ref, lt_ref, y_ref, z_s, sp_s, *, nb):
    u = u_ref[0]
    rows = u.shape[0]
    gl = SSM_T * SSM_GROUP
    sl = SSM_PAIR * SSM_STATE
    z_s[...] = jnp.dot(u, min_ref[0], preferred_element_type=F32)
    lt_re = jnp.broadcast_to(lt_ref[0, 0:1, :], (nb, sl))
    lt_im = jnp.broadcast_to(lt_ref[0, 1:2, :], (nb, sl))

    def step(c, carry):
        s_re, s_im = carry
        r0 = pl.multiple_of(c * nb, nb)
        sp_s[pl.ds(r0, nb), 0:sl] = s_re.astype(BF16)
        sp_s[pl.ds(r0, nb), sl:2 * sl] = s_im.astype(BF16)
        z = z_s[pl.ds(r0, nb), :]
        n_re = lt_re * s_re - lt_im * s_im + z[:, 0:sl]
        n_im = lt_re * s_im + lt_im * s_re + z[:, sl:2 * sl]
        return n_re, n_im

    zero = jnp.zeros((nb, sl), F32)
    lax.fori_loop(0, rows // nb, step, (zero, zero))
    y = jnp.dot(sp_s[...], mout_ref[0], preferred_element_type=F32)
    for gi in range(SSM_PAIR):
        cols = slice(gi * gl, (gi + 1) * gl)
        yg = y[:, cols] + jnp.dot(u[:, cols], toep_ref[0, gi], preferred_element_type=F32)
        y_ref[0, :, cols] = jax.nn.gelu(yg, approximate=True).astype(BF16)


def _ssm_operators(a_re, a_im, log_dt, b_re, b_im, c_re, c_im, d_skip):
    hp = lax.Precision.HIGHEST
    g, p = a_re.shape
    hg = b_re.shape[-1]
    t_len = SSM_T
    dt = jnp.exp(log_dt.astype(F32))[:, None]
    are, aim = a_re.astype(F32) * dt, a_im.astype(F32) * dt
    tt = jnp.arange(t_len + 1, dtype=F32)[:, None, None]
    mag = jnp.exp(are[None] * tt)
    pw_re, pw_im = mag * jnp.cos(aim[None] * tt), mag * jnp.sin(aim[None] * tt)
    lb_re, lb_im = pw_re[1], pw_im[1]
    lr, li = a_re.astype(F32), a_im.astype(F32)
    den = lr * lr + li * li
    f_re = ((lb_re - 1.0) * lr + lb_im * li) / den
    f_im = (lb_im * lr - (lb_re - 1.0) * li) / den
    bb_re = f_re[..., None] * b_re - f_im[..., None] * b_im
    bb_im = f_re[..., None] * b_im + f_im[..., None] * b_re
    cr, ci = c_re.astype(F32), c_im.astype(F32)
    cb_re = cr[:, :, :, None] * bb_re[:, None] - ci[:, :, :, None] * bb_im[:, None]
    cb_im = cr[:, :, :, None] * bb_im[:, None] + ci[:, :, :, None] * bb_re[:, None]
    kern = (jnp.einsum('gapb,tgp->tgab', cb_re, pw_re[:t_len], precision=hp)
            - jnp.einsum('gapb,tgp->tgab', cb_im, pw_im[:t_len], precision=hp))
    kern = kern.at[0].add(d_skip.astype(F32)[:, :, None] * jnp.eye(hg, dtype=F32)[None])
    jj = jnp.arange(t_len)[:, None]
    ii = jnp.arange(t_len)[None, :]
    lag = ii - jj
    toep = jnp.where((lag >= 0)[:, :, None, None, None], kern[jnp.clip(lag, 0, t_len - 1)], 0.0)
    toep = jnp.transpose(toep, (2, 0, 4, 1, 3)).reshape(g, t_len * hg, t_len * hg)
    toep = toep.reshape(g // SSM_PAIR, SSM_PAIR, t_len * hg, t_len * hg)
    rev_re, rev_im = pw_re[:t_len][::-1], pw_im[:t_len][::-1]
    mi_re = rev_re[:, :, :, None] * bb_re[None] - rev_im[:, :, :, None] * bb_im[None]
    mi_im = rev_re[:, :, :, None] * bb_im[None] + rev_im[:, :, :, None] * bb_re[None]
    to_rows = lambda a: jnp.transpose(a, (1, 0, 3, 2)).reshape(g // SSM_PAIR, SSM_PAIR, t_len * hg, p)
    eye = jnp.eye(SSM_PAIR, dtype=F32)
    m_in = jnp.stack([to_rows(mi_re), to_rows(mi_im)], axis=3)
    m_in = m_in[:, :, :, :, None, :] * eye[None, :, None, None, :, None]
    m_in = m_in.reshape(g // SSM_PAIR, SSM_PAIR * t_len * hg, 2 * SSM_PAIR * p)
    nx_re, nx_im = pw_re[1:], pw_im[1:]
    mo_re = cr[None] * nx_re[:, :, None, :] - ci[None] * nx_im[:, :, None, :]
    mo_im = -(cr[None] * nx_im[:, :, None, :] + ci[None] * nx_re[:, :, None, :])
    to_cols = lambda a: jnp.transpose(a, (1, 3, 0, 2)).reshape(g // SSM_PAIR, SSM_PAIR, p, t_len * hg)
    m_out = jnp.stack([to_cols(mo_re), to_cols(mo_im)], axis=1)
    m_out = m_out[:, :, :, :, None, :] * eye[None, None, :, None, :, None]
    m_out = m_out.reshape(g // SSM_PAIR, 2 * SSM_PAIR * p, SSM_PAIR * t_len * hg)
    lt = jnp.stack([pw_re[t_len], pw_im[t_len]], axis=0)
    lt = jnp.transpose(lt.reshape(2, g // SSM_PAIR, SSM_PAIR * p), (1, 0, 2))
    return m_in.astype(BF16), toep.astype(BF16), m_out.astype(BF16), lt


def _ssm(us, b, s, ops):
    m_in, toep, m_out, lt = ops
    g2 = m_in.shape[0]
    nc = s // SSM_T
    rows = nc * b
    wl = SSM_PAIR * SSM_T * SSM_GROUP
    sl2 = 2 * SSM_PAIR * SSM_STATE
    uf = us.reshape(b, nc, SSM_T, g2, SSM_PAIR, SSM_GROUP)
    uf = jnp.transpose(uf, (3, 1, 0, 4, 2, 5)).reshape(g2, rows, wl)
    per = lambda *shape: pl.BlockSpec((1,) + shape, lambda i: (i,) + (0,) * len(shape))
    y = pl.pallas_call(
        functools.partial(_ssm_kernel, nb=b),
        grid=(g2,),
        in_specs=[per(rows, wl), per(wl, sl2), per(SSM_PAIR, wl // SSM_PAIR, wl // SSM_PAIR),
                  per(sl2, wl), per(2, sl2 // 2)],
        out_specs=per(rows, wl),
        out_shape=jax.ShapeDtypeStruct((g2, rows, wl), BF16),
        scratch_shapes=[pltpu.VMEM((rows, sl2), F32), pltpu.VMEM((rows, sl2), BF16)],
        compiler_params=pltpu.CompilerParams(
            dimension_semantics=("arbitrary",), vmem_limit_bytes=VMEM_LIMIT),
        name="ssm_chunk",
    )(uf, m_in, toep, m_out, lt)
    y = y.reshape(g2, nc, b, SSM_PAIR, SSM_T, SSM_GROUP)
    return jnp.transpose(y, (2, 1, 4, 0, 3, 5)).reshape(b * s, g2 * SSM_PAIR * SSM_GROUP)


def _out_ffn_kernel(x_ref, att_ref, ys_ref, gate_ref, wa_ref, wg_ref, wb_ref, wo_ref,
                    gf_ref, w1_ref, w2_ref, gl_ref, o_ref):
    d = x_ref.shape[1]
    dot = lambda a, w: jnp.dot(a, w, preferred_element_type=F32)
    y_a = dot(att_ref[...], wa_ref[...])
    glu = dot(ys_ref[...], wg_ref[...])
    sw = glu.shape[1] // 2
    y_b = dot((glu[:, :sw] * jax.nn.sigmoid(glu[:, sw:])).astype(BF16), wb_ref[...])
    mixed = gate_ref[:, :d].astype(F32) * y_a + gate_ref[:, d:].astype(F32) * y_b
    h = x_ref[...] + dot(mixed.astype(BF16), wo_ref[...])
    n = _rms(h, gf_ref[...]).astype(BF16)
    dff = w1_ref.shape[1]
    fc = min(dff, 1024)
    acc = h
    for lo in range(0, dff, fc):
        f = jnp.maximum(dot(n, w1_ref[:, lo:lo + fc]), 0.0)
        acc = acc + dot((f * f).astype(BF16), w2_ref[lo:lo + fc, :])
    o_ref[...] = _rms(acc, gl_ref[...])


def _out_ffn(x2, att, ys, gates, w_a, w_glu, w_b, w_out, norm_ffn, w1, w2, norm_final, tm):
    m, d = x2.shape
    row = lambda w: pl.BlockSpec((tm, w), lambda i: (i, 0))
    weights = [w_a, w_glu, w_b, w_out]
    bf = [w.astype(BF16) for w in weights]
    w1b, w2b = w1.astype(BF16), w2.astype(BF16)
    return pl.pallas_call(
        _out_ffn_kernel,
        grid=(m // tm,),
        in_specs=[row(d), row(att.shape[1]), row(ys.shape[1]), row(gates.shape[1])]
        + [_const_spec(w.shape) for w in bf]
        + [_const_spec((1, d)), _const_spec(w1b.shape), _const_spec(w2b.shape), _const_spec((1, d))],
        out_specs=row(d),
        out_shape=jax.ShapeDtypeStruct((m, d), F32),
        compiler_params=pltpu.CompilerParams(
            dimension_semantics=("arbitrary",), vmem_limit_bytes=VMEM_LIMIT),
        name="out_ffn",
    )(x2, att, ys, gates, *bf, norm_ffn.reshape(1, d), w1b, w2b, norm_final.reshape(1, d))


def _token_block(m):
    for tm in (512, 256, 128):
        if m % tm == 0:
            return tm
    raise ValueError(f"token count {m} must be a multiple of 128")


def kernel(x, norm_mix, w_in, b_gate, rel_bias, ssm_a_re, ssm_a_im, ssm_log_dt, ssm_b_re, ssm_b_im,
           ssm_c_re, ssm_c_im, ssm_d, w_glu, w_proj_a, w_proj_b, w_out, norm_ffn, w_ff1, w_ff2,
           norm_final):
    b, s, d = x.shape
    assert norm_mix.shape[0] == 1, "single-layer block"
    assert s % Q_BLOCK == 0 and s >= KEY_WINDOW and s % SSM_T == 0
    assert b % 16 == 0, "batch rows fill whole bf16 sublane tiles in the SSM state scan"
    m = b * s
    tm = _token_block(m)
    x2 = x.reshape(m, d)
    q, k, v, us, gates = _in_proj(x2, norm_mix[0], w_in[0], b_gate[0], tm)
    att = _band_attn(q.reshape(b, s, -1), k.reshape(b, s, -1), v.reshape(b, s, -1), rel_bias[0])
    ops = _ssm_operators(ssm_a_re[0], ssm_a_im[0], ssm_log_dt[0], ssm_b_re[0], ssm_b_im[0],
                         ssm_c_re[0], ssm_c_im[0], ssm_d[0])
    ys = _ssm(us, b, s, ops)
    out = _out_ffn(x2, att.reshape(m, -1), ys, gates, w_proj_a[0], w_glu[0], w_proj_b[0], w_out[0],
                   norm_ffn[0], w_ff1[0], w_ff2[0], norm_final, tm)
    return out.reshape(b, s, d)
```

```python
import functools
import math

import jax
import jax.numpy as jnp
from jax import lax
from jax.experimental import pallas as pl
from jax.experimental.pallas import tpu as pltpu

F32 = jnp.float32
BF16 = jnp.bfloat16

CHUNK = 64
N_LEFT_CHUNKS = 8
ATT_HEADS = 8
HEAD_DIM = 64
ATT_WIDTH = ATT_HEADS * HEAD_DIM
REL_CLIP = 128
SSM_GROUP = 16
SSM_STATE = 64
EPS = 1e-6
NEG_INF = -1e30

LANES = 128
Q_BLOCK = 2 * CHUNK
KEY_WINDOW = Q_BLOCK + N_LEFT_CHUNKS * CHUNK
LEFT_BLOCKS = (N_LEFT_CHUNKS * CHUNK) // Q_BLOCK
SSM_T = 16
SSM_PAIR = 2
PIECES = LANES // SSM_GROUP
SSM_ROWS = 512
VMEM_LIMIT = 56 * 1024 * 1024


def _largest_divisor(n, cap):
    return max(d for d in range(1, min(n, cap) + 1) if n % d == 0)


def _const_spec(shape):
    nd = len(shape)
    return pl.BlockSpec(shape, lambda *_: (0,) * nd, pipeline_mode=pl.Buffered(1))


def _rms(x, g):
    ms = jnp.mean(x * x, axis=-1, keepdims=True)
    return x * lax.rsqrt(ms + EPS) * g


def _in_proj_kernel(x_ref, g_ref, w_ref, bg_ref, q_ref, k_ref, v_ref, us_ref, gate_ref):
    u = _rms(x_ref[...], g_ref[...]).astype(BF16)

    def proj(lo, hi):
        return jnp.dot(u, w_ref[:, lo:hi], preferred_element_type=F32)

    aw = ATT_WIDTH
    q_ref[...] = (proj(0, aw) * (HEAD_DIM ** -0.5)).astype(BF16)
    k_ref[...] = proj(aw, 2 * aw).astype(BF16)
    v_ref[...] = proj(2 * aw, 3 * aw).astype(BF16)
    sw = us_ref.shape[1]
    us_ref[...] = proj(3 * aw, 3 * aw + sw).astype(BF16)
    g0 = 3 * aw + sw
    gw = gate_ref.shape[1]
    half = gw // 2
    for lo in (0, half):
        z = proj(g0 + lo, g0 + lo + half) + bg_ref[:, lo:lo + half]
        gate_ref[:, lo:lo + half] = jax.nn.sigmoid(z).astype(BF16)


def _in_proj(x2, norm_g, w_in, b_gate, tm):
    m, d = x2.shape
    n_in = w_in.shape[1]
    gw = b_gate.shape[0]
    sw = n_in - 3 * ATT_WIDTH - gw
    row = lambda w: pl.BlockSpec((tm, w), lambda i: (i, 0))
    return pl.pallas_call(
        _in_proj_kernel,
        grid=(m // tm,),
        in_specs=[row(d), _const_spec((1, d)), _const_spec((d, n_in)), _const_spec((1, gw))],
        out_specs=[row(ATT_WIDTH), row(ATT_WIDTH), row(ATT_WIDTH), row(sw), row(gw)],
        out_shape=[jax.ShapeDtypeStruct((m, ATT_WIDTH), BF16)] * 3
        + [jax.ShapeDtypeStruct((m, sw), BF16), jax.ShapeDtypeStruct((m, gw), BF16)],
        compiler_params=pltpu.CompilerParams(
            dimension_semantics=("arbitrary",), vmem_limit_bytes=VMEM_LIMIT),
        name="in_proj",
    )(x2, norm_g.reshape(1, d), w_in.astype(BF16), b_gate.reshape(1, gw))


def _attn_kernel(q_ref, k_ref, v_ref, bias_ref, o_ref):
    i = pl.program_id(1)
    start = pl.multiple_of(jnp.maximum(i - LEFT_BLOCKS, 0) * Q_BLOCK, Q_BLOCK)
    first_half = lax.broadcasted_iota(jnp.int32, (Q_BLOCK, LANES), 1) < HEAD_DIM
    for p in range(ATT_HEADS // 2):
        cols = slice(p * LANES, (p + 1) * LANES)
        qp = q_ref[0, :, cols]
        kp = k_ref[0, pl.ds(start, KEY_WINDOW), cols]
        vp = v_ref[0, pl.ds(start, KEY_WINDOW), cols]
        outs = []
        for e in range(2):
            qm = jnp.where(first_half if e == 0 else jnp.logical_not(first_half), qp, jnp.zeros_like(qp))
            s = lax.dot_general(qm, kp, (((1,), (1,)), ((), ())), preferred_element_type=F32)
            s = s + bias_ref[0, 2 * p + e]
            mx = jnp.max(s, axis=-1, keepdims=True)
            pe = jnp.exp(s - mx)
            den = jnp.sum(pe, axis=-1, keepdims=True)
            o = jnp.dot(pe.astype(BF16), vp, preferred_element_type=F32)
            outs.append(o / den)
        o_ref[0, :, cols] = jnp.where(first_half, outs[0], outs[1]).astype(BF16)


def _attn_bias_tables(rel_bias):
    nt = LEFT_BLOCKS + 1
    h = rel_bias.shape[0]
    t = jnp.arange(nt)[:, None, None]
    qr = jnp.arange(Q_BLOCK)[None, :, None]
    r = jnp.arange(KEY_WINDOW)[None, None, :]
    qpos = t * Q_BLOCK + qr
    kpos = jnp.maximum(t - LEFT_BLOCKS, 0) * Q_BLOCK + r
    dchunk = qpos // CHUNK - kpos // CHUNK
    valid = (dchunk >= 0) & (dchunk <= N_LEFT_CHUNKS)
    span = KEY_WINDOW + Q_BLOCK
    left = KEY_WINDOW - 1 - REL_CLIP
    right = (nt - 1) * Q_BLOCK + span - left - (2 * REL_CLIP + 1)
    ext = jnp.pad(rel_bias.astype(F32), ((0, 0), (left, right)), mode="edge")
    tabs = []
    for ti in range(nt):
        v = ext[:, ti * Q_BLOCK: ti * Q_BLOCK + span]
        skew = jnp.tile(v, (1, Q_BLOCK + 1))[:, :Q_BLOCK * (span + 1)].reshape(h, Q_BLOCK, span + 1)
        tabs.append(skew[:, :, :KEY_WINDOW][:, :, ::-1])
    bias = jnp.stack(tabs)
    return jnp.where(valid[:, None], bias, NEG_INF)


def _band_attn(q, k, v, rel_bias):
    b, s, w = q.shape
    nq = s // Q_BLOCK
    bias = _attn_bias_tables(rel_bias)
    whole = pl.BlockSpec((1, s, w), lambda bi, i: (bi, 0, 0))
    return pl.pallas_call(
        _attn_kernel,
        grid=(b, nq),
        in_specs=[
            pl.BlockSpec((1, Q_BLOCK, w), lambda bi, i: (bi, i, 0)),
            whole, whole,
            pl.BlockSpec((1, ATT_HEADS, Q_BLOCK, KEY_WINDOW),
                         lambda bi, i: (jnp.minimum(i, LEFT_BLOCKS), 0, 0, 0)),
        ],
        out_specs=pl.BlockSpec((1, Q_BLOCK, w), lambda bi, i: (bi, i, 0)),
        out_shape=jax.ShapeDtypeStruct((b, s, w), BF16),
        compiler_params=pltpu.CompilerParams(
            dimension_semantics=("arbitrary", "arbitrary"), vmem_limit_bytes=VMEM_LIMIT),
        name="band_attn",
    )(q, k, v, bias)


def _piece_transpose(tiles):
    lane = lax.broadcasted_iota(jnp.int32, tiles[0].shape, 1)
    t = list(tiles)
    for s in (4, 2, 1):
        keep = ((lane // SSM_GROUP) & s) == 0
        sh = s * SSM_GROUP
        nxt = list(t)
        for i in range(PIECES):
            if i & s == 0:
                a, b = t[i], t[i + s]
                nxt[i] = jnp.where(keep, a, pltpu.roll(b, sh, 1))
                nxt[i + s] = jnp.where(keep, pltpu.roll(a, LANES - sh, 1), b)
        t = nxt
    return t


def _ssm_kernel(u_ref, min_ref, toep_ref, mout_ref, lt_ref, y_ref, st_s, ub_s, z_s, sp_s, yb_s, *, nb):
    npair = min_ref.shape[0]
    rows = u_ref.shape[2]
    sl = SSM_PAIR * SSM_STATE
    gl = SSM_T * SSM_GROUP

    @pl.when(pl.program_id(1) == 0)
    def _():
        st_s[...] = jnp.zeros_like(st_s)

    for jo in range(SSM_T // PIECES):
        tiles = [u_ref[0, jo * PIECES + j8].astype(F32) for j8 in range(PIECES)]
        for g, tile in enumerate(_piece_transpose(tiles)):
            lo = (g % SSM_PAIR) * gl + jo * LANES
            ub_s[g // SSM_PAIR, :, lo:lo + LANES] = tile.astype(BF16)

    for pr in range(npair):
        z_s[pr] = jnp.dot(ub_s[pr], min_ref[pr], preferred_element_type=F32)

    lts = [(jnp.broadcast_to(lt_ref[pr, 0:1, :], (nb, sl)), jnp.broadcast_to(lt_ref[pr, 1:2, :], (nb, sl)))
           for pr in range(npair)]

    def step(c, carry):
        r0 = pl.multiple_of(c * nb, nb)
        out = []
        for pr in range(npair):
            s_re, s_im = carry[2 * pr], carry[2 * pr + 1]
            lt_re, lt_im = lts[pr]
            sp_s[pr, pl.ds(r0, nb), 0:sl] = s_re.astype(BF16)
            sp_s[pr, pl.ds(r0, nb), sl:2 * sl] = s_im.astype(BF16)
            z = z_s[pr, pl.ds(r0, nb), :]
            out.append(lt_re * s_re - lt_im * s_im + z[:, 0:sl])
            out.append(lt_re * s_im + lt_im * s_re + z[:, sl:2 * sl])
        return tuple(out)

    init = tuple(st_s[pr, part] for pr in range(npair) for part in range(2))
    fin = lax.fori_loop(0, rows // nb, step, init)
    for pr in range(npair):
        st_s[pr, 0] = fin[2 * pr]
        st_s[pr, 1] = fin[2 * pr + 1]

    for pr in range(npair):
        y = jnp.dot(sp_s[pr], mout_ref[pr], preferred_element_type=F32)
        for gi in range(SSM_PAIR):
            cols = slice(gi * gl, (gi + 1) * gl)
            yg = y[:, cols] + jnp.dot(ub_s[pr, :, cols], toep_ref[pr, gi], preferred_element_type=F32)
            yb_s[pr, :, cols] = jax.nn.gelu(yg, approximate=True)

    for jo in range(SSM_T // PIECES):
        tiles = []
        for g in range(PIECES):
            lo = (g % SSM_PAIR) * gl + jo * LANES
            tiles.append(yb_s[g // SSM_PAIR, :, lo:lo + LANES])
        for j8, tile in enumerate(_piece_transpose(tiles)):
            y_ref[0, jo * PIECES + j8] = tile.astype(BF16)


def _ssm_operators(a_re, a_im, log_dt, b_re, b_im, c_re, c_im, d_skip):
    hp = lax.Precision.HIGHEST
    g, p = a_re.shape
    hg = b_re.shape[-1]
    t_len = SSM_T
    dt = jnp.exp(log_dt.astype(F32))[:, None]
    are, aim = a_re.astype(F32) * dt, a_im.astype(F32) * dt
    tt = jnp.arange(t_len + 1, dtype=F32)[:, None, None]
    mag = jnp.exp(are[None] * tt)
    pw_re, pw_im = mag * jnp.cos(aim[None] * tt), mag * jnp.sin(aim[None] * tt)
    lb_re, lb_im = pw_re[1], pw_im[1]
    lr, li = a_re.astype(F32), a_im.astype(F32)
    den = lr * lr + li * li
    f_re = ((lb_re - 1.0) * lr + lb_im * li) / den
    f_im = (lb_im * lr - (lb_re - 1.0) * li) / den
    bb_re = f_re[..., None] * b_re - f_im[..., None] * b_im
    bb_im = f_re[..., None] * b_im + f_im[..., None] * b_re
    cr, ci = c_re.astype(F32), c_im.astype(F32)
    cb_re = cr[:, :, :, None] * bb_re[:, None] - ci[:, :, :, None] * bb_im[:, None]
    cb_im = cr[:, :, :, None] * bb_im[:, None] + ci[:, :, :, None] * bb_re[:, None]
    kern = (jnp.einsum('gapb,tgp->tgab', cb_re, pw_re[:t_len], precision=hp)
            - jnp.einsum('gapb,tgp->tgab', cb_im, pw_im[:t_len], precision=hp))
    kern = kern.at[0].add(d_skip.astype(F32)[:, :, None] * jnp.eye(hg, dtype=F32)[None])
    jj = jnp.arange(t_len)[:, None]
    ii = jnp.arange(t_len)[None, :]
    lag = ii - jj
    toep = jnp.where((lag >= 0)[:, :, None, None, None], kern[jnp.clip(lag, 0, t_len - 1)], 0.0)
    toep = jnp.transpose(toep, (2, 0, 4, 1, 3)).reshape(g, t_len * hg, t_len * hg)
    toep = toep.reshape(g // SSM_PAIR, SSM_PAIR, t_len * hg, t_len * hg)
    rev_re, rev_im = pw_re[:t_len][::-1], pw_im[:t_len][::-1]
    mi_re = rev_re[:, :, :, None] * bb_re[None] - rev_im[:, :, :, None] * bb_im[None]
    mi_im = rev_re[:, :, :, None] * bb_im[None] + rev_im[:, :, :, None] * bb_re[None]
    to_rows = lambda a: jnp.transpose(a, (1, 0, 3, 2)).reshape(g // SSM_PAIR, SSM_PAIR, t_len * hg, p)
    eye = jnp.eye(SSM_PAIR, dtype=F32)
    m_in = jnp.stack([to_rows(mi_re), to_rows(mi_im)], axis=3)
    m_in = m_in[:, :, :, :, None, :] * eye[None, :, None, None, :, None]
    m_in = m_in.reshape(g // SSM_PAIR, SSM_PAIR * t_len * hg, 2 * SSM_PAIR * p)
    nx_re, nx_im = pw_re[1:], pw_im[1:]
    mo_re = cr[None] * nx_re[:, :, None, :] - ci[None] * nx_im[:, :, None, :]
    mo_im = -(cr[None] * nx_im[:, :, None, :] + ci[None] * nx_re[:, :, None, :])
    to_cols = lambda a: jnp.transpose(a, (1, 3, 0, 2)).reshape(g // SSM_PAIR, SSM_PAIR, p, t_len * hg)
    m_out = jnp.stack([to_cols(mo_re), to_cols(mo_im)], axis=1)
    m_out = m_out[:, :, :, :, None, :] * eye[None, None, :, None, :, None]
    m_out = m_out.reshape(g // SSM_PAIR, 2 * SSM_PAIR * p, SSM_PAIR * t_len * hg)
    lt = jnp.stack([pw_re[t_len], pw_im[t_len]], axis=0)
    lt = jnp.transpose(lt.reshape(2, g // SSM_PAIR, SSM_PAIR * p), (1, 0, 2))
    return m_in.astype(BF16), toep.astype(BF16), m_out.astype(BF16), lt


def _ssm(us, b, s, ops):
    m_in, toep, m_out, lt = ops
    nc = s // SSM_T
    sw = us.shape[1]
    nbd = sw // LANES
    npair = PIECES // SSM_PAIR
    wl = SSM_PAIR * SSM_T * SSM_GROUP
    sl2 = 2 * SSM_PAIR * SSM_STATE
    cb = _largest_divisor(nc, SSM_ROWS // b)
    rows = cb * b
    uf = jnp.transpose(us.reshape(b, nc, SSM_T, nbd, LANES), (3, 2, 1, 0, 4)).reshape(nbd, SSM_T, nc * b, LANES)
    per = lambda *shape: pl.BlockSpec((npair,) + shape, lambda bd, cr: (bd,) + (0,) * len(shape))
    io = pl.BlockSpec((1, SSM_T, rows, LANES), lambda bd, cr: (bd, 0, cr, 0))
    y = pl.pallas_call(
        functools.partial(_ssm_kernel, nb=b),
        grid=(nbd, nc // cb),
        in_specs=[io, per(wl, sl2), per(SSM_PAIR, wl // SSM_PAIR, wl // SSM_PAIR), per(sl2, wl),
                  per(2, sl2 // 2)],
        out_specs=io,
        out_shape=jax.ShapeDtypeStruct(uf.shape, BF16),
        scratch_shapes=[pltpu.VMEM((npair, 2, b, sl2 // 2), F32), pltpu.VMEM((npair, rows, wl), BF16),
                        pltpu.VMEM((npair, rows, sl2), F32), pltpu.VMEM((npair, rows, sl2), BF16),
                        pltpu.VMEM((npair, rows, wl), F32)],
        compiler_params=pltpu.CompilerParams(
            dimension_semantics=("arbitrary", "arbitrary"), vmem_limit_bytes=VMEM_LIMIT),
        name="ssm_chunk",
    )(uf, m_in, toep, m_out, lt)
    y = jnp.transpose(y.reshape(nbd, SSM_T, nc, b, LANES), (3, 2, 1, 0, 4))
    return y.reshape(b * s, sw)


def _out_ffn_kernel(x_ref, att_ref, ys_ref, gate_ref, wa_ref, wg_ref, wb_ref, wo_ref,
                    gf_ref, w1_ref, w2_ref, gl_ref, o_ref):
    d = x_ref.shape[1]
    dot = lambda a, w: jnp.dot(a, w, preferred_element_type=F32)
    y_a = dot(att_ref[...], wa_ref[...])
    glu = dot(ys_ref[...], wg_ref[...])
    sw = glu.shape[1] // 2
    y_b = dot((glu[:, :sw] * jax.nn.sigmoid(glu[:, sw:])).astype(BF16), wb_ref[...])
    mixed = gate_ref[:, :d].astype(F32) * y_a + gate_ref[:, d:].astype(F32) * y_b
    h = x_ref[...] + dot(mixed.astype(BF16), wo_ref[...])
    n = _rms(h, gf_ref[...]).astype(BF16)
    dff = w1_ref.shape[1]
    fc = min(dff, 1024)
    acc = h
    for lo in range(0, dff, fc):
        f = jnp.maximum(dot(n, w1_ref[:, lo:lo + fc]), 0.0)
        acc = acc + dot((f * f).astype(BF16), w2_ref[lo:lo + fc, :])
    o_ref[...] = _rms(acc, gl_ref[...])


def _out_ffn(x2, att, ys, gates, w_a, w_glu, w_b, w_out, norm_ffn, w1, w2, norm_final, tm):
    m, d = x2.shape
    row = lambda w: pl.BlockSpec((tm, w), lambda i: (i, 0))
    weights = [w_a, w_glu, w_b, w_out]
    bf = [w.astype(BF16) for w in weights]
    w1b, w2b = w1.astype(BF16), w2.astype(BF16)
    return pl.pallas_call(
        _out_ffn_kernel,
        grid=(m // tm,),
        in_specs=[row(d), row(att.shape[1]), row(ys.shape[1]), row(gates.shape[1])]
        + [_const_spec(w.shape) for w in bf]
        + [_const_spec((1, d)), _const_spec(w1b.shape), _const_spec(w2b.shape), _const_spec((1, d))],
        out_specs=row(d),
        out_shape=jax.ShapeDtypeStruct((m, d), F32),
        compiler_params=pltpu.CompilerParams(
            dimension_semantics=("arbitrary",), vmem_limit_bytes=VMEM_LIMIT),
        name="out_ffn",
    )(x2, att, ys, gates, *bf, norm_ffn.reshape(1, d), w1b, w2b, norm_final.reshape(1, d))


def _token_block(m):
    for tm in (512, 256, 128):
        if m % tm == 0:
            return tm
    raise ValueError(f"token count {m} must be a multiple of 128")


def kernel(x, norm_mix, w_in, b_gate, rel_bias, ssm_a_re, ssm_a_im, ssm_log_dt, ssm_b_re, ssm_b_im,
           ssm_c_re, ssm_c_im, ssm_d, w_glu, w_proj_a, w_proj_b, w_out, norm_ffn, w_ff1, w_ff2,
           norm_final):
    b, s, d = x.shape
    assert norm_mix.shape[0] == 1, "single-layer block"
    assert s % Q_BLOCK == 0 and s >= KEY_WINDOW and s % SSM_T == 0
    assert b % 16 == 0, "batch rows fill whole bf16 sublane tiles in the SSM state scan"
    m = b * s
    tm = _token_block(m)
    x2 = x.reshape(m, d)
    q, k, v, us, gates = _in_proj(x2, norm_mix[0], w_in[0], b_gate[0], tm)
    att = _band_attn(q.reshape(b, s, -1), k.reshape(b, s, -1), v.reshape(b, s, -1), rel_bias[0])
    ops = _ssm_operators(ssm_a_re[0], ssm_a_im[0], ssm_log_dt[0], ssm_b_re[0], ssm_b_im[0],
                         ssm_c_re[0], ssm_c_im[0], ssm_d[0])
    ys = _ssm(us, b, s, ops)
    out = _out_ffn(x2, att.reshape(m, -1), ys, gates, w_proj_a[0], w_glu[0], w_proj_b[0], w_out[0],
                   norm_ffn[0], w_ff1[0], w_ff2[0], norm_final, tm)
    return out.reshape(b, s, d)
```

```python
import functools
import math

import jax
import jax.numpy as jnp
from jax import lax
from jax.experimental import pallas as pl
from jax.experimental.pallas import tpu as pltpu

F32 = jnp.float32
BF16 = jnp.bfloat16

CHUNK = 64
N_LEFT_CHUNKS = 8
ATT_HEADS = 8
HEAD_DIM = 64
ATT_WIDTH = ATT_HEADS * HEAD_DIM
REL_CLIP = 128
SSM_GROUP = 16
SSM_STATE = 64
EPS = 1e-6
NEG_INF = -1e30

LANES = 128
Q_BLOCK = 4 * CHUNK
KEY_WINDOW = Q_BLOCK + N_LEFT_CHUNKS * CHUNK
LEFT_BLOCKS = (N_LEFT_CHUNKS * CHUNK) // Q_BLOCK
BLOCK_DISTANCES = (N_LEFT_CHUNKS * CHUNK) // LANES + 1
SSM_T = 16
SSM_PAIR = 2
PIECES = LANES // SSM_GROUP
SSM_ROWS = 512
VMEM_LIMIT = 56 * 1024 * 1024


def _largest_divisor(n, cap):
    return max(d for d in range(1, min(n, cap) + 1) if n % d == 0)


def _const_spec(shape):
    nd = len(shape)
    return pl.BlockSpec(shape, lambda *_: (0,) * nd, pipeline_mode=pl.Buffered(1))


def _rms(x, g):
    ms = jnp.mean(x * x, axis=-1, keepdims=True)
    return x * lax.rsqrt(ms + EPS) * g


def _in_proj_kernel(x_ref, g_ref, w_ref, bg_ref, q_ref, k_ref, v_ref, us_ref, gate_ref):
    u = _rms(x_ref[...], g_ref[...]).astype(BF16)

    def proj(lo, hi):
        return jnp.dot(u, w_ref[:, lo:hi], preferred_element_type=F32)

    aw = ATT_WIDTH
    q_ref[...] = (proj(0, aw) * (HEAD_DIM ** -0.5)).astype(BF16)
    k_ref[...] = proj(aw, 2 * aw).astype(BF16)
    v_ref[...] = proj(2 * aw, 3 * aw).astype(BF16)
    sw = us_ref.shape[1]
    us_ref[...] = proj(3 * aw, 3 * aw + sw).astype(BF16)
    g0 = 3 * aw + sw
    gw = gate_ref.shape[1]
    half = gw // 2
    for lo in (0, half):
        z = proj(g0 + lo, g0 + lo + half) + bg_ref[:, lo:lo + half]
        gate_ref[:, lo:lo + half] = jax.nn.sigmoid(z).astype(BF16)


def _in_proj(x2, norm_g, w_in, b_gate, tm):
    m, d = x2.shape
    n_in = w_in.shape[1]
    gw = b_gate.shape[0]
    sw = n_in - 3 * ATT_WIDTH - gw
    row = lambda w: pl.BlockSpec((tm, w), lambda i: (i, 0))
    return pl.pallas_call(
        _in_proj_kernel,
        grid=(m // tm,),
        in_specs=[row(d), _const_spec((1, d)), _const_spec((d, n_in)), _const_spec((1, gw))],
        out_specs=[row(ATT_WIDTH), row(ATT_WIDTH), row(ATT_WIDTH), row(sw), row(gw)],
        out_shape=[jax.ShapeDtypeStruct((m, ATT_WIDTH), BF16)] * 3
        + [jax.ShapeDtypeStruct((m, sw), BF16), jax.ShapeDtypeStruct((m, gw), BF16)],
        compiler_params=pltpu.CompilerParams(
            dimension_semantics=("arbitrary",), vmem_limit_bytes=VMEM_LIMIT),
        name="in_proj",
    )(x2, norm_g.reshape(1, d), w_in.astype(BF16), b_gate.reshape(1, gw))


def _build_bias_tiles(wrev_ref, tiles_s):
    qi = lax.broadcasted_iota(jnp.int32, (LANES, LANES), 0)
    ri = lax.broadcasted_iota(jnp.int32, (LANES, LANES), 1)
    below = ri <= qi
    dchunk = qi // CHUNK - ri // CHUNK
    per_block = LANES // CHUNK
    for h in range(ATT_HEADS):
        for d in range(BLOCK_DISTANCES):
            row = wrev_ref[pl.ds(h * BLOCK_DISTANCES + d, 1), :]
            lo = pltpu.roll(jnp.broadcast_to(row[:, :LANES], (LANES, LANES)), 1, 1, stride=1, stride_axis=0)
            hi = pltpu.roll(jnp.broadcast_to(row[:, LANES:], (LANES, LANES)), 1, 1, stride=1, stride_axis=0)
            dc = per_block * d + dchunk
            valid = (dc >= 0) & (dc <= N_LEFT_CHUNKS)
            tiles_s[d, h] = jnp.where(valid, jnp.where(below, lo, hi), NEG_INF)
        tiles_s[BLOCK_DISTANCES, h] = jnp.full((LANES, LANES), NEG_INF, F32)


def _attn_kernel(q_ref, k_ref, v_ref, wrev_ref, o_ref, tiles_s):
    i = pl.program_id(1)

    @pl.when((pl.program_id(0) == 0) & (i == 0))
    def _():
        _build_bias_tiles(wrev_ref, tiles_s)

    start = pl.multiple_of(jnp.maximum(i - LEFT_BLOCKS, 0) * Q_BLOCK, Q_BLOCK)
    base = jnp.minimum(i, LEFT_BLOCKS) * (Q_BLOCK // LANES)
    first_half = lax.broadcasted_iota(jnp.int32, (Q_BLOCK, LANES), 1) < HEAD_DIM

    def tile_index(qs, kb):
        d = base + qs - kb
        return jnp.where((d >= 0) & (d < BLOCK_DISTANCES), d, BLOCK_DISTANCES)

    for p in range(ATT_HEADS // 2):
        cols = slice(p * LANES, (p + 1) * LANES)
        qp = q_ref[0, :, cols]
        zero = jnp.zeros_like(qp)
        qm = jnp.concatenate([jnp.where(first_half, qp, zero), jnp.where(first_half, zero, qp)], axis=0)
        kp = k_ref[0, pl.ds(start, KEY_WINDOW), cols]
        vp = v_ref[0, pl.ds(start, KEY_WINDOW), cols]
        s = lax.dot_general(qm, kp, (((1,), (1,)), ((), ())), preferred_element_type=F32)
        bias = jnp.concatenate(
            [jnp.concatenate([tiles_s[tile_index(qs, kb), 2 * p + e] for kb in range(KEY_WINDOW // LANES)], axis=1)
             for e in range(2) for qs in range(Q_BLOCK // LANES)], axis=0)
        s = s + bias
        mx = jnp.max(s, axis=-1, keepdims=True)
        pe = jnp.exp(s - mx)
        den = jnp.sum(pe, axis=-1, keepdims=True)
        o = jnp.dot(pe.astype(BF16), vp, preferred_element_type=F32) / den
        o_ref[0, :, cols] = jnp.where(first_half, o[:Q_BLOCK], o[Q_BLOCK:]).astype(BF16)


def _reversed_bias_windows(rel_bias):
    d = jnp.arange(BLOCK_DISTANCES)[:, None]
    m = jnp.arange(2 * LANES)[None, :]
    idx = jnp.clip(LANES * d + LANES - 1 - m, -REL_CLIP, REL_CLIP) + REL_CLIP
    return rel_bias.astype(F32)[:, idx].reshape(ATT_HEADS * BLOCK_DISTANCES, 2 * LANES)


def _band_attn(q, k, v, rel_bias):
    b, s, w = q.shape
    nq = s // Q_BLOCK
    wrev = _reversed_bias_windows(rel_bias)
    whole = pl.BlockSpec((1, s, w), lambda bi, i: (bi, 0, 0))
    return pl.pallas_call(
        _attn_kernel,
        grid=(b, nq),
        in_specs=[
            pl.BlockSpec((1, Q_BLOCK, w), lambda bi, i: (bi, i, 0)),
            whole, whole,
            pl.BlockSpec(wrev.shape, lambda bi, i: (0, 0)),
        ],
        out_specs=pl.BlockSpec((1, Q_BLOCK, w), lambda bi, i: (bi, i, 0)),
        out_shape=jax.ShapeDtypeStruct((b, s, w), BF16),
        scratch_shapes=[pltpu.VMEM((BLOCK_DISTANCES + 1, ATT_HEADS, LANES, LANES), F32)],
        compiler_params=pltpu.CompilerParams(
            dimension_semantics=("arbitrary", "arbitrary"), vmem_limit_bytes=VMEM_LIMIT),
        name="band_attn",
    )(q, k, v, wrev)


def _piece_transpose(tiles):
    lane = lax.broadcasted_iota(jnp.int32, tiles[0].shape, 1)
    t = list(tiles)
    for s in (4, 2, 1):
        keep = ((lane // SSM_GROUP) & s) == 0
        sh = s * SSM_GROUP
        nxt = list(t)
        for i in range(PIECES):
            if i & s == 0:
                a, b = t[i], t[i + s]
                nxt[i] = jnp.where(keep, a, pltpu.roll(b, sh, 1))
                nxt[i + s] = jnp.where(keep, pltpu.roll(a, LANES - sh, 1), b)
        t = nxt
    return t


def _ssm_kernel(u_ref, min_ref, toep_ref, mout_ref, lt_ref, y_ref, st_s, ub_s, z_s, sp_s, yb_s, *, nb):
    npair = min_ref.shape[0]
    rows = u_ref.shape[2]
    sl = SSM_PAIR * SSM_STATE
    gl = SSM_T * SSM_GROUP

    @pl.when(pl.program_id(1) == 0)
    def _():
        st_s[...] = jnp.zeros_like(st_s)

    for jo in range(SSM_T // PIECES):
        tiles = [u_ref[0, jo * PIECES + j8].astype(F32) for j8 in range(PIECES)]
        for g, tile in enumerate(_piece_transpose(tiles)):
            lo = (g % SSM_PAIR) * gl + jo * LANES
            ub_s[g // SSM_PAIR, :, lo:lo + LANES] = tile.astype(BF16)

    for pr in range(npair):
        z_s[pr] = jnp.dot(ub_s[pr], min_ref[pr], preferred_element_type=F32)

    lts = [(jnp.broadcast_to(lt_ref[pr, 0:1, :], (nb, sl)), jnp.broadcast_to(lt_ref[pr, 1:2, :], (nb, sl)))
           for pr in range(npair)]

    def step(c, carry):
        r0 = pl.multiple_of(c * nb, nb)
        out = []
        for pr in range(npair):
            s_re, s_im = carry[2 * pr], carry[2 * pr + 1]
            lt_re, lt_im = lts[pr]
            sp_s[pr, pl.ds(r0, nb), 0:sl] = s_re.astype(BF16)
            sp_s[pr, pl.ds(r0, nb), sl:2 * sl] = s_im.astype(BF16)
            z = z_s[pr, pl.ds(r0, nb), :]
            out.append(lt_re * s_re - lt_im * s_im + z[:, 0:sl])
            out.append(lt_re * s_im + lt_im * s_re + z[:, sl:2 * sl])
        return tuple(out)

    init = tuple(st_s[pr, part] for pr in range(npair) for part in range(2))
    fin = lax.fori_loop(0, rows // nb, step, init)
    for pr in range(npair):
        st_s[pr, 0] = fin[2 * pr]
        st_s[pr, 1] = fin[2 * pr + 1]

    for pr in range(npair):
        y = jnp.dot(sp_s[pr], mout_ref[pr], preferred_element_type=F32)
        for gi in range(SSM_PAIR):
            cols = slice(gi * gl, (gi + 1) * gl)
            yg = y[:, cols] + jnp.dot(ub_s[pr, :, cols], toep_ref[pr, gi], preferred_element_type=F32)
            yb_s[pr, :, cols] = jax.nn.gelu(yg, approximate=True)

    for jo in range(SSM_T // PIECES):
        tiles = []
        for g in range(PIECES):
            lo = (g % SSM_PAIR) * gl + jo * LANES
            tiles.append(yb_s[g // SSM_PAIR, :, lo:lo + LANES])
        for j8, tile in enumerate(_piece_transpose(tiles)):
            y_ref[0, jo * PIECES + j8] = tile.astype(BF16)


def _ssm_operators(a_re, a_im, log_dt, b_re, b_im, c_re, c_im, d_skip):
    hp = lax.Precision.HIGHEST
    g, p = a_re.shape
    hg = b_re.shape[-1]
    t_len = SSM_T
    dt = jnp.exp(log_dt.astype(F32))[:, None]
    are, aim = a_re.astype(F32) * dt, a_im.astype(F32) * dt
    tt = jnp.arange(t_len + 1, dtype=F32)[:, None, None]
    mag = jnp.exp(are[None] * tt)
    pw_re, pw_im = mag * jnp.cos(aim[None] * tt), mag * jnp.sin(aim[None] * tt)
    lb_re, lb_im = pw_re[1], pw_im[1]
    lr, li = a_re.astype(F32), a_im.astype(F32)
    den = lr * lr + li * li
    f_re = ((lb_re - 1.0) * lr + lb_im * li) / den
    f_im = (lb_im * lr - (lb_re - 1.0) * li) / den
    bb_re = f_re[..., None] * b_re - f_im[..., None] * b_im
    bb_im = f_re[..., None] * b_im + f_im[..., None] * b_re
    cr, ci = c_re.astype(F32), c_im.astype(F32)
    cb_re = cr[:, :, :, None] * bb_re[:, None] - ci[:, :, :, None] * bb_im[:, None]
    cb_im = cr[:, :, :, None] * bb_im[:, None] + ci[:, :, :, None] * bb_re[:, None]
    kern = (jnp.einsum('gapb,tgp->tgab', cb_re, pw_re[:t_len], precision=hp)
            - jnp.einsum('gapb,tgp->tgab', cb_im, pw_im[:t_len], precision=hp))
    kern = kern.at[0].add(d_skip.astype(F32)[:, :, None] * jnp.eye(hg, dtype=F32)[None])
    jj = jnp.arange(t_len)[:, None]
    ii = jnp.arange(t_len)[None, :]
    lag = ii - jj
    toep = jnp.where((lag >= 0)[:, :, None, None, None], kern[jnp.clip(lag, 0, t_len - 1)], 0.0)
    toep = jnp.transpose(toep, (2, 0, 4, 1, 3)).reshape(g, t_len * hg, t_len * hg)
    toep = toep.reshape(g // SSM_PAIR, SSM_PAIR, t_len * hg, t_len * hg)
    rev_re, rev_im = pw_re[:t_len][::-1], pw_im[:t_len][::-1]
    mi_re = rev_re[:, :, :, None] * bb_re[None] - rev_im[:, :, :, None] * bb_im[None]
    mi_im = rev_re[:, :, :, None] * bb_im[None] + rev_im[:, :, :, None] * bb_re[None]
    to_rows = lambda a: jnp.transpose(a, (1, 0, 3, 2)).reshape(g // SSM_PAIR, SSM_PAIR, t_len * hg, p)
    eye = jnp.eye(SSM_PAIR, dtype=F32)
    m_in = jnp.stack([to_rows(mi_re), to_rows(mi_im)], axis=3)
    m_in = m_in[:, :, :, :, None, :] * eye[None, :, None, None, :, None]
    m_in = m_in.reshape(g // SSM_PAIR, SSM_PAIR * t_len * hg, 2 * SSM_PAIR * p)
    nx_re, nx_im = pw_re[1:], pw_im[1:]
    mo_re = cr[None] * nx_re[:, :, None, :] - ci[None] * nx_im[:, :, None, :]
    mo_im = -(cr[None] * nx_im[:, :, None, :] + ci[None] * nx_re[:, :, None, :])
    to_cols = lambda a: jnp.transpose(a, (1, 3, 0, 2)).reshape(g // SSM_PAIR, SSM_PAIR, p, t_len * hg)
    m_out = jnp.stack([to_cols(mo_re), to_cols(mo_im)], axis=1)
    m_out = m_out[:, :, :, :, None, :] * eye[None, None, :, None, :, None]
    m_out = m_out.reshape(g // SSM_PAIR, 2 * SSM_PAIR * p, SSM_PAIR * t_len * hg)
    lt = jnp.stack([pw_re[t_len], pw_im[t_len]], axis=0)
    lt = jnp.transpose(lt.reshape(2, g // SSM_PAIR, SSM_PAIR * p), (1, 0, 2))
    return m_in.astype(BF16), toep.astype(BF16), m_out.astype(BF16), lt


def _ssm(us, b, s, ops):
    m_in, toep, m_out, lt = ops
    nc = s // SSM_T
    sw = us.shape[1]
    nbd = sw // LANES
    npair = PIECES // SSM_PAIR
    wl = SSM_PAIR * SSM_T * SSM_GROUP
    sl2 = 2 * SSM_PAIR * SSM_STATE
    cb = _largest_divisor(nc, SSM_ROWS // b)
    rows = cb * b
    uf = jnp.transpose(us.reshape(b, nc, SSM_T, nbd, LANES), (3, 2, 1, 0, 4)).reshape(nbd, SSM_T, nc * b, LANES)
    per = lambda *shape: pl.BlockSpec((npair,) + shape, lambda bd, cr: (bd,) + (0,) * len(shape))
    io = pl.BlockSpec((1, SSM_T, rows, LANES), lambda bd, cr: (bd, 0, cr, 0))
    y = pl.pallas_call(
        functools.partial(_ssm_kernel, nb=b),
        grid=(nbd, nc // cb),
        in_specs=[io, per(wl, sl2), per(SSM_PAIR, wl // SSM_PAIR, wl // SSM_PAIR), per(sl2, wl),
                  per(2, sl2 // 2)],
        out_specs=io,
        out_shape=jax.ShapeDtypeStruct(uf.shape, BF16),
        scratch_shapes=[pltpu.VMEM((npair, 2, b, sl2 // 2), F32), pltpu.VMEM((npair, rows, wl), BF16),
                        pltpu.VMEM((npair, rows, sl2), F32), pltpu.VMEM((npair, rows, sl2), BF16),
                        pltpu.VMEM((npair, rows, wl), F32)],
        compiler_params=pltpu.CompilerParams(
            dimension_semantics=("arbitrary", "arbitrary"), vmem_limit_bytes=VMEM_LIMIT),
        name="ssm_chunk",
    )(uf, m_in, toep, m_out, lt)
    y = jnp.transpose(y.reshape(nbd, SSM_T, nc, b, LANES), (3, 2, 1, 0, 4))
    return y.reshape(b * s, sw)


def _out_ffn_kernel(x_ref, att_ref, ys_ref, gate_ref, wa_ref, wg_ref, wb_ref, wo_ref,
                    gf_ref, w1_ref, w2_ref, gl_ref, o_ref):
    d = x_ref.shape[1]
    dot = lambda a, w: jnp.dot(a, w, preferred_element_type=F32)
    y_a = dot(att_ref[...], wa_ref[...])
    glu = dot(ys_ref[...], wg_ref[...])
    sw = glu.shape[1] // 2
    y_b = dot((glu[:, :sw] * jax.nn.sigmoid(glu[:, sw:])).astype(BF16), wb_ref[...])
    mixed = gate_ref[:, :d].astype(F32) * y_a + gate_ref[:, d:].astype(F32) * y_b
    h = x_ref[...] + dot(mixed.astype(BF16), wo_ref[...])
    n = _rms(h, gf_ref[...]).astype(BF16)
    dff = w1_ref.shape[1]
    fc = min(dff, 1024)
    acc = h
    for lo in range(0, dff, fc):
        f = jnp.maximum(dot(n, w1_ref[:, lo:lo + fc]), 0.0)
        acc = acc + dot((f * f).astype(BF16), w2_ref[lo:lo + fc, :])
    o_ref[...] = _rms(acc, gl_ref[...])


def _out_ffn(x2, att, ys, gates, w_a, w_glu, w_b, w_out, norm_ffn, w1, w2, norm_final, tm):
    m, d = x2.shape
    row = lambda w: pl.BlockSpec((tm, w), lambda i: (i, 0))
    weights = [w_a, w_glu, w_b, w_out]
    bf = [w.astype(BF16) for w in weights]
    w1b, w2b = w1.astype(BF16), w2.astype(BF16)
    return pl.pallas_call(
        _out_ffn_kernel,
        grid=(m // tm,),
        in_specs=[row(d), row(att.shape[1]), row(ys.shape[1]), row(gates.shape[1])]
        + [_const_spec(w.shape) for w in bf]
        + [_const_spec((1, d)), _const_spec(w1b.shape), _const_spec(w2b.shape), _const_spec((1, d))],
        out_specs=row(d),
        out_shape=jax.ShapeDtypeStruct((m, d), F32),
        compiler_params=pltpu.CompilerParams(
            dimension_semantics=("arbitrary",), vmem_limit_bytes=VMEM_LIMIT),
        name="out_ffn",
    )(x2, att, ys, gates, *bf, norm_ffn.reshape(1, d), w1b, w2b, norm_final.reshape(1, d))


def _token_block(m):
    for tm in (512, 256, 128):
        if m % tm == 0:
            return tm
    raise ValueError(f"token count {m} must be a multiple of 128")


def kernel(x, norm_mix, w_in, b_gate, rel_bias, ssm_a_re, ssm_a_im, ssm_log_dt, ssm_b_re, ssm_b_im,
           ssm_c_re, ssm_c_im, ssm_d, w_glu, w_proj_a, w_proj_b, w_out, norm_ffn, w_ff1, w_ff2,
           norm_final):
    b, s, d = x.shape
    assert norm_mix.shape[0] == 1, "single-layer block"
    assert s % Q_BLOCK == 0 and s >= KEY_WINDOW and s % SSM_T == 0
    assert b % 16 == 0, "batch rows fill whole bf16 sublane tiles in the SSM state scan"
    m = b * s
    tm = _token_block(m)
    x2 = x.reshape(m, d)
    q, k, v, us, gates = _in_proj(x2, norm_mix[0], w_in[0], b_gate[0], tm)
    att = _band_attn(q.reshape(b, s, -1), k.reshape(b, s, -1), v.reshape(b, s, -1), rel_bias[0])
    ops = _ssm_operators(ssm_a_re[0], ssm_a_im[0], ssm_log_dt[0], ssm_b_re[0], ssm_b_im[0],
                         ssm_c_re[0], ssm_c_im[0], ssm_d[0])
    ys = _ssm(us, b, s, ops)
    out = _out_ffn(x2, att.reshape(m, -1), ys, gates, w_proj_a[0], w_glu[0], w_proj_b[0], w_out[0],
                   norm_ffn[0], w_ff1[0], w_ff2[0], norm_final, tm)
    return out.reshape(b, s, d)
```

```python
import functools
import math

import jax
import jax.numpy as jnp
from jax import lax
from jax.experimental import pallas as pl
from jax.experimental.pallas import tpu as pltpu

F32 = jnp.float32
BF16 = jnp.bfloat16

CHUNK = 64
N_LEFT_CHUNKS = 8
ATT_HEADS = 8
HEAD_DIM = 64
ATT_WIDTH = ATT_HEADS * HEAD_DIM
REL_CLIP = 128
SSM_GROUP = 16
SSM_STATE = 64
EPS = 1e-6
NEG_INF = -1e30

LANES = 128
Q_BLOCK = 4 * CHUNK
KEY_WINDOW = Q_BLOCK + N_LEFT_CHUNKS * CHUNK
LEFT_BLOCKS = (N_LEFT_CHUNKS * CHUNK) // Q_BLOCK
BLOCK_DISTANCES = (N_LEFT_CHUNKS * CHUNK) // LANES + 1
SSM_T = 16
SSM_PAIR = 2
PIECES = LANES // SSM_GROUP
SSM_ROWS = 512
TOKEN_BLOCK = 512
VMEM_LIMIT = 56 * 1024 * 1024


def _largest_divisor(n, cap):
    return max(d for d in range(1, min(n, cap) + 1) if n % d == 0)


def _row_pitch(ts):
    assert ts % 8 == 0
    return ts + 4


def _const_spec(shape):
    nd = len(shape)
    return pl.BlockSpec(shape, lambda *_: (0,) * nd, pipeline_mode=pl.Buffered(1))


def _rms(x, g):
    ms = jnp.mean(x * x, axis=-1, keepdims=True)
    return x * lax.rsqrt(ms + EPS) * g


def _in_proj_kernel(x_ref, g_ref, w_ref, bg_ref, q_ref, k_ref, v_ref, us_ref, gate_ref, us_s):
    nb, ts, d = x_ref.shape
    u = _rms(x_ref[...].reshape(nb * ts, d), g_ref[...]).astype(BF16)

    def proj(lo, hi):
        return jnp.dot(u, w_ref[:, lo:hi], preferred_element_type=F32)

    aw = ATT_WIDTH
    q_ref[...] = (proj(0, aw) * (HEAD_DIM ** -0.5)).astype(BF16).reshape(nb, ts, aw)
    k_ref[...] = proj(aw, 2 * aw).astype(BF16).reshape(nb, ts, aw)
    v_ref[...] = proj(2 * aw, 3 * aw).astype(BF16).reshape(nb, ts, aw)
    nbd = us_s.shape[0]
    pitch = us_s.shape[1] // nb
    us = proj(3 * aw, 3 * aw + nbd * LANES)
    for bd in range(nbd):
        for bi in range(nb):
            us_s[bd, bi * pitch:bi * pitch + ts, :] = us[bi * ts:(bi + 1) * ts, bd * LANES:(bd + 1) * LANES]
    for bd in range(nbd):
        for t in range(ts):
            c, j = divmod(t, SSM_T)
            us_ref[bd, j, c * nb:(c + 1) * nb, :] = us_s[bd, pl.ds(t, nb, stride=pitch), :].astype(BF16)
    g0 = 3 * aw + nbd * LANES
    gw = gate_ref.shape[2]
    half = gw // 2
    for lo in (0, half):
        z = proj(g0 + lo, g0 + lo + half) + bg_ref[:, lo:lo + half]
        gate_ref[:, :, lo:lo + half] = jax.nn.sigmoid(z).astype(BF16).reshape(nb, ts, half)


def _in_proj(x, norm_g, w_in, b_gate, ts):
    b, s, d = x.shape
    n_in = w_in.shape[1]
    gw = b_gate.shape[0]
    sw = n_in - 3 * ATT_WIDTH - gw
    nbd = sw // LANES
    row = lambda w: pl.BlockSpec((b, ts, w), lambda i: (0, i, 0))
    ssm_rows = (ts // SSM_T) * b
    return pl.pallas_call(
        _in_proj_kernel,
        grid=(s // ts,),
        in_specs=[row(d), _const_spec((1, d)), _const_spec((d, n_in)), _const_spec((1, gw))],
        out_specs=[row(ATT_WIDTH), row(ATT_WIDTH), row(ATT_WIDTH),
                   pl.BlockSpec((nbd, SSM_T, ssm_rows, LANES), lambda i: (0, 0, i, 0)), row(gw)],
        out_shape=[jax.ShapeDtypeStruct((b, s, ATT_WIDTH), BF16)] * 3
        + [jax.ShapeDtypeStruct((nbd, SSM_T, (s // SSM_T) * b, LANES), BF16),
           jax.ShapeDtypeStruct((b, s, gw), BF16)],
        scratch_shapes=[pltpu.VMEM((nbd, b * _row_pitch(ts), LANES), F32)],
        compiler_params=pltpu.CompilerParams(
            dimension_semantics=("arbitrary",), vmem_limit_bytes=VMEM_LIMIT),
        name="in_proj",
    )(x, norm_g.reshape(1, d), w_in.astype(BF16), b_gate.reshape(1, gw))


def _build_bias_tiles(wrev_ref, tiles_s):
    qi = lax.broadcasted_iota(jnp.int32, (LANES, LANES), 0)
    ri = lax.broadcasted_iota(jnp.int32, (LANES, LANES), 1)
    below = ri <= qi
    dchunk = qi // CHUNK - ri // CHUNK
    per_block = LANES // CHUNK
    for h in range(ATT_HEADS):
        for d in range(BLOCK_DISTANCES):
            row = wrev_ref[pl.ds(h * BLOCK_DISTANCES + d, 1), :]
            lo = pltpu.roll(jnp.broadcast_to(row[:, :LANES], (LANES, LANES)), 1, 1, stride=1, stride_axis=0)
            hi = pltpu.roll(jnp.broadcast_to(row[:, LANES:], (LANES, LANES)), 1, 1, stride=1, stride_axis=0)
            dc = per_block * d + dchunk
            valid = (dc >= 0) & (dc <= N_LEFT_CHUNKS)
            tiles_s[d, h] = jnp.where(valid, jnp.where(below, lo, hi), NEG_INF)
        tiles_s[BLOCK_DISTANCES, h] = jnp.full((LANES, LANES), NEG_INF, F32)


def _attn_kernel(q_ref, k_ref, v_ref, wrev_ref, o_ref, tiles_s):
    i = pl.program_id(1)

    @pl.when((pl.program_id(0) == 0) & (i == 0))
    def _():
        _build_bias_tiles(wrev_ref, tiles_s)

    start = pl.multiple_of(jnp.maximum(i - LEFT_BLOCKS, 0) * Q_BLOCK, Q_BLOCK)
    base = jnp.minimum(i, LEFT_BLOCKS) * (Q_BLOCK // LANES)
    first_half = lax.broadcasted_iota(jnp.int32, (Q_BLOCK, LANES), 1) < HEAD_DIM

    def tile_index(qs, kb):
        d = base + qs - kb
        return jnp.where((d >= 0) & (d < BLOCK_DISTANCES), d, BLOCK_DISTANCES)

    for p in range(ATT_HEADS // 2):
        cols = slice(p * LANES, (p + 1) * LANES)
        qp = q_ref[0, :, cols]
        zero = jnp.zeros_like(qp)
        qm = jnp.concatenate([jnp.where(first_half, qp, zero), jnp.where(first_half, zero, qp)], axis=0)
        kp = k_ref[0, pl.ds(start, KEY_WINDOW), cols]
        vp = v_ref[0, pl.ds(start, KEY_WINDOW), cols]
        s = lax.dot_general(qm, kp, (((1,), (1,)), ((), ())), preferred_element_type=F32)
        bias = jnp.concatenate(
            [jnp.concatenate([tiles_s[tile_index(qs, kb), 2 * p + e] for kb in range(KEY_WINDOW // LANES)], axis=1)
             for e in range(2) for qs in range(Q_BLOCK // LANES)], axis=0)
        s = s + bias
        mx = jnp.max(s, axis=-1, keepdims=True)
        pe = jnp.exp(s - mx)
        den = jnp.sum(pe, axis=-1, keepdims=True)
        o = jnp.dot(pe.astype(BF16), vp, preferred_element_type=F32) / den
        o_ref[0, :, cols] = jnp.where(first_half, o[:Q_BLOCK], o[Q_BLOCK:]).astype(BF16)


def _reversed_bias_windows(rel_bias):
    d = jnp.arange(BLOCK_DISTANCES)[:, None]
    m = jnp.arange(2 * LANES)[None, :]
    idx = jnp.clip(LANES * d + LANES - 1 - m, -REL_CLIP, REL_CLIP) + REL_CLIP
    return rel_bias.astype(F32)[:, idx].reshape(ATT_HEADS * BLOCK_DISTANCES, 2 * LANES)


def _band_attn(q, k, v, rel_bias):
    b, s, w = q.shape
    nq = s // Q_BLOCK
    wrev = _reversed_bias_windows(rel_bias)
    whole = pl.BlockSpec((1, s, w), lambda bi, i: (bi, 0, 0))
    return pl.pallas_call(
        _attn_kernel,
        grid=(b, nq),
        in_specs=[
            pl.BlockSpec((1, Q_BLOCK, w), lambda bi, i: (bi, i, 0)),
            whole, whole,
            pl.BlockSpec(wrev.shape, lambda bi, i: (0, 0)),
        ],
        out_specs=pl.BlockSpec((1, Q_BLOCK, w), lambda bi, i: (bi, i, 0)),
        out_shape=jax.ShapeDtypeStruct((b, s, w), BF16),
        scratch_shapes=[pltpu.VMEM((BLOCK_DISTANCES + 1, ATT_HEADS, LANES, LANES), F32)],
        compiler_params=pltpu.CompilerParams(
            dimension_semantics=("arbitrary", "arbitrary"), vmem_limit_bytes=VMEM_LIMIT),
        name="band_attn",
    )(q, k, v, wrev)


def _piece_transpose(tiles):
    lane = lax.broadcasted_iota(jnp.int32, tiles[0].shape, 1)
    t = list(tiles)
    for s in (4, 2, 1):
        keep = ((lane // SSM_GROUP) & s) == 0
        sh = s * SSM_GROUP
        nxt = list(t)
        for i in range(PIECES):
            if i & s == 0:
                a, b = t[i], t[i + s]
                nxt[i] = jnp.where(keep, a, pltpu.roll(b, sh, 1))
                nxt[i + s] = jnp.where(keep, pltpu.roll(a, LANES - sh, 1), b)
        t = nxt
    return t


def _ssm_kernel(u_ref, min_ref, toep_ref, mout_ref, lt_ref, y_ref, st_s, ub_s, z_s, sp_s, yb_s, *, nb):
    npair = min_ref.shape[0]
    rows = u_ref.shape[2]
    sl = SSM_PAIR * SSM_STATE
    gl = SSM_T * SSM_GROUP

    @pl.when(pl.program_id(1) == 0)
    def _():
        st_s[...] = jnp.zeros_like(st_s)

    for jo in range(SSM_T // PIECES):
        tiles = [u_ref[0, jo * PIECES + j8].astype(F32) for j8 in range(PIECES)]
        for g, tile in enumerate(_piece_transpose(tiles)):
            lo = (g % SSM_PAIR) * gl + jo * LANES
            ub_s[g // SSM_PAIR, :, lo:lo + LANES] = tile.astype(BF16)

    for pr in range(npair):
        z_s[pr] = jnp.dot(ub_s[pr], min_ref[pr], preferred_element_type=F32)

    lts = [(jnp.broadcast_to(lt_ref[pr, 0:1, :], (nb, sl)), jnp.broadcast_to(lt_ref[pr, 1:2, :], (nb, sl)))
           for pr in range(npair)]

    def step(c, carry):
        r0 = pl.multiple_of(c * nb, nb)
        out = []
        for pr in range(npair):
            s_re, s_im = carry[2 * pr], carry[2 * pr + 1]
            lt_re, lt_im = lts[pr]
            sp_s[pr, pl.ds(r0, nb), 0:sl] = s_re.astype(BF16)
            sp_s[pr, pl.ds(r0, nb), sl:2 * sl] = s_im.astype(BF16)
            z = z_s[pr, pl.ds(r0, nb), :]
            out.append(lt_re * s_re - lt_im * s_im + z[:, 0:sl])
            out.append(lt_re * s_im + lt_im * s_re + z[:, sl:2 * sl])
        return tuple(out)

    init = tuple(st_s[pr, part] for pr in range(npair) for part in range(2))
    fin = lax.fori_loop(0, rows // nb, step, init)
    for pr in range(npair):
        st_s[pr, 0] = fin[2 * pr]
        st_s[pr, 1] = fin[2 * pr + 1]

    for pr in range(npair):
        y = jnp.dot(sp_s[pr], mout_ref[pr], preferred_element_type=F32)
        for gi in range(SSM_PAIR):
            cols = slice(gi * gl, (gi + 1) * gl)
            yg = y[:, cols] + jnp.dot(ub_s[pr, :, cols], toep_ref[pr, gi], preferred_element_type=F32)
            yb_s[pr, :, cols] = jax.nn.gelu(yg, approximate=True)

    for jo in range(SSM_T // PIECES):
        tiles = []
        for g in range(PIECES):
            lo = (g % SSM_PAIR) * gl + jo * LANES
            tiles.append(yb_s[g // SSM_PAIR, :, lo:lo + LANES])
        for j8, tile in enumerate(_piece_transpose(tiles)):
            y_ref[0, jo * PIECES + j8] = tile.astype(BF16)


def _ssm_operators(a_re, a_im, log_dt, b_re, b_im, c_re, c_im, d_skip):
    hp = lax.Precision.HIGHEST
    g, p = a_re.shape
    hg = b_re.shape[-1]
    t_len = SSM_T
    g2 = g // SSM_PAIR
    dt = jnp.exp(log_dt.astype(F32))[:, None]
    are, aim = a_re.astype(F32) * dt, a_im.astype(F32) * dt
    tt = jnp.arange(t_len + 1, dtype=F32)
    mag = jnp.exp(are[:, :, None] * tt)
    pw_re, pw_im = mag * jnp.cos(aim[:, :, None] * tt), mag * jnp.sin(aim[:, :, None] * tt)
    lb_re, lb_im = pw_re[:, :, 1], pw_im[:, :, 1]
    lr, li = a_re.astype(F32), a_im.astype(F32)
    den = lr * lr + li * li
    f_re = ((lb_re - 1.0) * lr + lb_im * li) / den
    f_im = (lb_im * lr - (lb_re - 1.0) * li) / den
    bb_re = f_re[..., None] * b_re - f_im[..., None] * b_im
    bb_im = f_re[..., None] * b_im + f_im[..., None] * b_re
    crt = jnp.transpose(c_re.astype(F32), (0, 2, 1))
    cit = jnp.transpose(c_im.astype(F32), (0, 2, 1))
    cpw_re = crt[:, :, None, :] * pw_re[..., None] - cit[:, :, None, :] * pw_im[..., None]
    cpw_im = crt[:, :, None, :] * pw_im[..., None] + cit[:, :, None, :] * pw_re[..., None]
    kern = jnp.einsum('gqta,gqb->tgab',
                      jnp.concatenate([cpw_re[:, :, :t_len], -cpw_im[:, :, :t_len]], axis=1),
                      jnp.concatenate([bb_re, bb_im], axis=1), precision=hp)
    lag0 = (jnp.arange(t_len) == 0).astype(F32)[:, None, None, None]
    kern = kern + lag0 * (d_skip.astype(F32)[:, :, None] * jnp.eye(hg, dtype=F32))[None]
    jj = jnp.arange(t_len)[None, :, None]
    ii = jnp.arange(t_len)[None, None, :]
    shift = (ii - jj == jnp.arange(t_len)[:, None, None]).astype(F32)
    toep = jnp.einsum('tji,tgab->gjbia', shift, kern, precision=hp)
    toep = toep.reshape(g2, SSM_PAIR, t_len * hg, t_len * hg)
    eye = jnp.eye(SSM_PAIR, dtype=F32)
    rev_re = jnp.transpose(pw_re[:, :, :t_len][:, :, ::-1], (0, 2, 1))[:, :, None, :]
    rev_im = jnp.transpose(pw_im[:, :, :t_len][:, :, ::-1], (0, 2, 1))[:, :, None, :]
    bt_re = jnp.transpose(bb_re, (0, 2, 1))[:, None]
    bt_im = jnp.transpose(bb_im, (0, 2, 1))[:, None]
    m_in = jnp.stack([rev_re * bt_re - rev_im * bt_im, rev_re * bt_im + rev_im * bt_re], axis=3)
    m_in = m_in.reshape(g2, SSM_PAIR, t_len, hg, 2, 1, p) * eye[None, :, None, None, None, :, None]
    m_in = m_in.reshape(g2, SSM_PAIR * t_len * hg, 2 * SSM_PAIR * p)
    m_out = jnp.stack([cpw_re[:, :, 1:], -cpw_im[:, :, 1:]], axis=0)
    m_out = jnp.transpose(m_out.reshape(2, g2, SSM_PAIR, p, 1, t_len, hg), (1, 0, 2, 3, 4, 5, 6))
    m_out = m_out * eye[None, None, :, None, :, None, None]
    m_out = m_out.reshape(g2, 2 * SSM_PAIR * p, SSM_PAIR * t_len * hg)
    lt = jnp.stack([pw_re[:, :, t_len], pw_im[:, :, t_len]], axis=0)
    lt = jnp.transpose(lt.reshape(2, g2, SSM_PAIR * p), (1, 0, 2))
    return m_in.astype(BF16), toep.astype(BF16), m_out.astype(BF16), lt


def _ssm(uf, b, s, ops):
    m_in, toep, m_out, lt = ops
    nc = s // SSM_T
    nbd = uf.shape[0]
    npair = PIECES // SSM_PAIR
    wl = SSM_PAIR * SSM_T * SSM_GROUP
    sl2 = 2 * SSM_PAIR * SSM_STATE
    cb = _largest_divisor(nc, SSM_ROWS // b)
    rows = cb * b
    per = lambda *shape: pl.BlockSpec((npair,) + shape, lambda bd, cr: (bd,) + (0,) * len(shape))
    io = pl.BlockSpec((1, SSM_T, rows, LANES), lambda bd, cr: (bd, 0, cr, 0))
    return pl.pallas_call(
        functools.partial(_ssm_kernel, nb=b),
        grid=(nbd, nc // cb),
        in_specs=[io, per(wl, sl2), per(SSM_PAIR, wl // SSM_PAIR, wl // SSM_PAIR), per(sl2, wl),
                  per(2, sl2 // 2)],
        out_specs=io,
        out_shape=jax.ShapeDtypeStruct(uf.shape, BF16),
        scratch_shapes=[pltpu.VMEM((npair, 2, b, sl2 // 2), F32), pltpu.VMEM((npair, rows, wl), BF16),
                        pltpu.VMEM((npair, rows, sl2), F32), pltpu.VMEM((npair, rows, sl2), BF16),
                        pltpu.VMEM((npair, rows, wl), F32)],
        compiler_params=pltpu.CompilerParams(
            dimension_semantics=("arbitrary", "arbitrary"), vmem_limit_bytes=VMEM_LIMIT),
        name="ssm_chunk",
    )(uf, m_in, toep, m_out, lt)


def _out_ffn_kernel(x_ref, att_ref, ys_ref, gate_ref, wa_ref, wg_ref, wb_ref, wo_ref,
                    gf_ref, w1_ref, w2_ref, gl_ref, o_ref, ys_s):
    nb, ts, d = x_ref.shape
    rows = nb * ts
    dot = lambda a, w: jnp.dot(a, w, preferred_element_type=F32)
    nbd = ys_s.shape[0]
    pitch = ys_s.shape[1] // nb
    for bd in range(nbd):
        for t in range(ts):
            c, j = divmod(t, SSM_T)
            ys_s[bd, pl.ds(t, nb, stride=pitch), :] = ys_ref[bd, j, c * nb:(c + 1) * nb, :].astype(F32)
    ys = jnp.concatenate(
        [jnp.concatenate([ys_s[bd, bi * pitch:bi * pitch + ts, :] for bi in range(nb)], axis=0)
         for bd in range(nbd)], axis=1).astype(BF16)
    y_a = dot(att_ref[...].reshape(rows, att_ref.shape[2]), wa_ref[...])
    glu = dot(ys, wg_ref[...])
    sw = glu.shape[1] // 2
    y_b = dot((glu[:, :sw] * jax.nn.sigmoid(glu[:, sw:])).astype(BF16), wb_ref[...])
    gates = gate_ref[...].reshape(rows, 2 * d)
    mixed = gates[:, :d].astype(F32) * y_a + gates[:, d:].astype(F32) * y_b
    h = x_ref[...].reshape(rows, d) + dot(mixed.astype(BF16), wo_ref[...])
    n = _rms(h, gf_ref[...]).astype(BF16)
    dff = w1_ref.shape[1]
    fc = min(dff, 1024)
    acc = h
    for lo in range(0, dff, fc):
        f = jnp.maximum(dot(n, w1_ref[:, lo:lo + fc]), 0.0)
        acc = acc + dot((f * f).astype(BF16), w2_ref[lo:lo + fc, :])
    o_ref[...] = _rms(acc, gl_ref[...]).reshape(nb, ts, d)


def _out_ffn(x, att, ys, gates, w_a, w_glu, w_b, w_out, norm_ffn, w1, w2, norm_final, ts):
    b, s, d = x.shape
    nbd = ys.shape[0]
    row = lambda w: pl.BlockSpec((b, ts, w), lambda i: (0, i, 0))
    weights = [w_a, w_glu, w_b, w_out]
    bf = [w.astype(BF16) for w in weights]
    w1b, w2b = w1.astype(BF16), w2.astype(BF16)
    return pl.pallas_call(
        _out_ffn_kernel,
        grid=(s // ts,),
        in_specs=[row(d), row(att.shape[2]),
                  pl.BlockSpec((nbd, SSM_T, (ts // SSM_T) * b, LANES), lambda i: (0, 0, i, 0)),
                  row(gates.shape[2])]
        + [_const_spec(w.shape) for w in bf]
        + [_const_spec((1, d)), _const_spec(w1b.shape), _const_spec(w2b.shape), _const_spec((1, d))],
        out_specs=row(d),
        out_shape=jax.ShapeDtypeStruct((b, s, d), F32),
        scratch_shapes=[pltpu.VMEM((nbd, b * _row_pitch(ts), LANES), F32)],
        compiler_params=pltpu.CompilerParams(
            dimension_semantics=("arbitrary",), vmem_limit_bytes=VMEM_LIMIT),
        name="out_ffn",
    )(x, att, ys, gates, *bf, norm_ffn.reshape(1, d), w1b, w2b, norm_final.reshape(1, d))


def _time_block(b, s):
    ts = _largest_divisor(s // SSM_T, max(1, TOKEN_BLOCK // (b * SSM_T))) * SSM_T
    assert ts % 16 == 0, "bf16 blocks need whole 16-row sublane tiles"
    return ts


def kernel(x, norm_mix, w_in, b_gate, rel_bias, ssm_a_re, ssm_a_im, ssm_log_dt, ssm_b_re, ssm_b_im,
           ssm_c_re, ssm_c_im, ssm_d, w_glu, w_proj_a, w_proj_b, w_out, norm_ffn, w_ff1, w_ff2,
           norm_final):
    b, s, d = x.shape
    assert norm_mix.shape[0] == 1, "single-layer block"
    assert s % Q_BLOCK == 0 and s >= KEY_WINDOW and s % SSM_T == 0
    assert b % 16 == 0, "batch rows fill whole bf16 sublane tiles in the SSM state scan"
    ts = _time_block(b, s)
    q, k, v, us, gates = _in_proj(x, norm_mix[0], w_in[0], b_gate[0], ts)
    att = _band_attn(q, k, v, rel_bias[0])
    ops = _ssm_operators(ssm_a_re[0], ssm_a_im[0], ssm_log_dt[0], ssm_b_re[0], ssm_b_im[0],
                         ssm_c_re[0], ssm_c_im[0], ssm_d[0])
    ys = _ssm(us, b, s, ops)
    return _out_ffn(x, att, ys, gates, w_proj_a[0], w_glu[0], w_proj_b[0], w_out[0],
                    norm_ffn[0], w_ff1[0], w_ff2[0], norm_final, ts)
```

```python
import functools
import math

import jax
import jax.numpy as jnp
from jax import lax
from jax.experimental import pallas as pl
from jax.experimental.pallas import tpu as pltpu

F32 = jnp.float32
BF16 = jnp.bfloat16

CHUNK = 64
N_LEFT_CHUNKS = 8
ATT_HEADS = 8
HEAD_DIM = 64
ATT_WIDTH = ATT_HEADS * HEAD_DIM
REL_CLIP = 128
SSM_GROUP = 16
SSM_STATE = 64
EPS = 1e-6
NEG_INF = -1e30

LANES = 128
Q_BLOCK = 4 * CHUNK
KEY_WINDOW = Q_BLOCK + N_LEFT_CHUNKS * CHUNK
LEFT_BLOCKS = (N_LEFT_CHUNKS * CHUNK) // Q_BLOCK
BLOCK_DISTANCES = (N_LEFT_CHUNKS * CHUNK) // LANES + 1
SSM_T = 16
SSM_PAIR = 2
PIECES = LANES // SSM_GROUP
SSM_ROWS = 1024
IN_PROJ_ROWS = 1024
OUT_FFN_ROWS = 512
LOG2E = math.log2(math.e)
Q_SCALE = HEAD_DIM ** -0.5 * LOG2E
VMEM_LIMIT = 56 * 1024 * 1024


def _largest_divisor(n, cap):
    return max(d for d in range(1, min(n, cap) + 1) if n % d == 0)


def _row_pitch(ts):
    assert ts % 8 == 0
    return ts + 4


def _const_spec(shape):
    nd = len(shape)
    return pl.BlockSpec(shape, lambda *_: (0,) * nd, pipeline_mode=pl.Buffered(1))


def _rms(x, g):
    ms = jnp.mean(x * x, axis=-1, keepdims=True)
    return x * lax.rsqrt(ms + EPS) * g


def _in_proj_kernel(x_ref, g_ref, w_ref, bg_ref, q_ref, k_ref, v_ref, us_ref, gate_ref, us_s):
    nb, ts, d = x_ref.shape
    u = _rms(x_ref[...].reshape(nb * ts, d), g_ref[...]).astype(BF16)

    def proj(lo, hi):
        return jnp.dot(u, w_ref[:, lo:hi], preferred_element_type=F32)

    aw = ATT_WIDTH
    q_ref[...] = (proj(0, aw) * Q_SCALE).astype(BF16).reshape(nb, ts, aw)
    k_ref[...] = proj(aw, 2 * aw).astype(BF16).reshape(nb, ts, aw)
    v_ref[...] = proj(2 * aw, 3 * aw).astype(BF16).reshape(nb, ts, aw)
    nbd = us_s.shape[0]
    pitch = us_s.shape[1] // nb
    gl = SSM_T * SSM_GROUP
    us = proj(3 * aw, 3 * aw + nbd * LANES)
    for bd in range(nbd):
        for bi in range(nb):
            us_s[bd, bi * pitch:bi * pitch + ts, :] = us[bi * ts:(bi + 1) * ts, bd * LANES:(bd + 1) * LANES]
    for bd in range(nbd):
        for jo in range(SSM_T // PIECES):
            tiles = [jnp.concatenate([us_s[bd, pl.ds(c * SSM_T + jo * PIECES + j8, nb, stride=pitch), :]
                                      for c in range(ts // SSM_T)], axis=0) for j8 in range(PIECES)]
            for g, tile in enumerate(_piece_transpose(tiles)):
                lo = (g % SSM_PAIR) * gl + jo * LANES
                us_ref[bd * (PIECES // SSM_PAIR) + g // SSM_PAIR, :, lo:lo + LANES] = tile.astype(BF16)
    g0 = 3 * aw + nbd * LANES
    gw = gate_ref.shape[2]
    half = gw // 2
    for lo in (0, half):
        z = proj(g0 + lo, g0 + lo + half) + bg_ref[:, lo:lo + half]
        gate_ref[:, :, lo:lo + half] = jax.nn.sigmoid(z).astype(BF16).reshape(nb, ts, half)


def _in_proj(x, norm_g, w_in, b_gate, ts):
    b, s, d = x.shape
    n_in = w_in.shape[1]
    gw = b_gate.shape[0]
    sw = n_in - 3 * ATT_WIDTH - gw
    nbd = sw // LANES
    row = lambda w: pl.BlockSpec((b, ts, w), lambda i: (0, i, 0))
    ssm_rows = (ts // SSM_T) * b
    npairs = sw // (SSM_PAIR * SSM_GROUP)
    pair_lanes = SSM_PAIR * SSM_T * SSM_GROUP
    return pl.pallas_call(
        _in_proj_kernel,
        grid=(s // ts,),
        in_specs=[row(d), _const_spec((1, d)), _const_spec((d, n_in)), _const_spec((1, gw))],
        out_specs=[row(ATT_WIDTH), row(ATT_WIDTH), row(ATT_WIDTH),
                   pl.BlockSpec((npairs, ssm_rows, pair_lanes), lambda i: (0, i, 0)), row(gw)],
        out_shape=[jax.ShapeDtypeStruct((b, s, ATT_WIDTH), BF16)] * 3
        + [jax.ShapeDtypeStruct((npairs, (s // SSM_T) * b, pair_lanes), BF16),
           jax.ShapeDtypeStruct((b, s, gw), BF16)],
        scratch_shapes=[pltpu.VMEM((nbd, b * _row_pitch(ts), LANES), F32)],
        compiler_params=pltpu.CompilerParams(
            dimension_semantics=("arbitrary",), vmem_limit_bytes=VMEM_LIMIT),
        name="in_proj",
    )(x, norm_g.reshape(1, d), w_in.astype(BF16), b_gate.reshape(1, gw))


def _build_bias_tiles(wrev_ref, tiles_s):
    qi = lax.broadcasted_iota(jnp.int32, (LANES, LANES), 0)
    ri = lax.broadcasted_iota(jnp.int32, (LANES, LANES), 1)
    below = ri <= qi
    dchunk = qi // CHUNK - ri // CHUNK
    per_block = LANES // CHUNK
    for h in range(ATT_HEADS):
        for d in range(BLOCK_DISTANCES):
            row = wrev_ref[pl.ds(h * BLOCK_DISTANCES + d, 1), :]
            lo = pltpu.roll(jnp.broadcast_to(row[:, :LANES], (LANES, LANES)), 1, 1, stride=1, stride_axis=0)
            hi = pltpu.roll(jnp.broadcast_to(row[:, LANES:], (LANES, LANES)), 1, 1, stride=1, stride_axis=0)
            dc = per_block * d + dchunk
            valid = (dc >= 0) & (dc <= N_LEFT_CHUNKS)
            tiles_s[d, h] = jnp.where(valid, jnp.where(below, lo, hi), NEG_INF)
        tiles_s[BLOCK_DISTANCES, h] = jnp.full((LANES, LANES), NEG_INF, F32)


def _attn_kernel(q_ref, k_ref, v_ref, wrev_ref, o_ref, tiles_s):
    i = pl.program_id(1)

    @pl.when((pl.program_id(0) == 0) & (i == 0))
    def _():
        _build_bias_tiles(wrev_ref, tiles_s)

    start = pl.multiple_of(jnp.maximum(i - LEFT_BLOCKS, 0) * Q_BLOCK, Q_BLOCK)
    base = jnp.minimum(i, LEFT_BLOCKS) * (Q_BLOCK // LANES)
    first_half = lax.broadcasted_iota(jnp.int32, (Q_BLOCK, LANES), 1) < HEAD_DIM

    def tile_index(qs, kb):
        d = base + qs - kb
        return jnp.where((d >= 0) & (d < BLOCK_DISTANCES), d, BLOCK_DISTANCES)

    for p in range(ATT_HEADS // 2):
        cols = slice(p * LANES, (p + 1) * LANES)
        qp = q_ref[0, :, cols]
        zero = jnp.zeros_like(qp)
        qm = jnp.concatenate([jnp.where(first_half, qp, zero), jnp.where(first_half, zero, qp)], axis=0)
        kp = k_ref[0, pl.ds(start, KEY_WINDOW), cols]
        vp = v_ref[0, pl.ds(start, KEY_WINDOW), cols]
        s = lax.dot_general(qm, kp, (((1,), (1,)), ((), ())), preferred_element_type=F32)
        bias = jnp.concatenate(
            [jnp.concatenate([tiles_s[tile_index(qs, kb), 2 * p + e] for kb in range(KEY_WINDOW // LANES)], axis=1)
             for e in range(2) for qs in range(Q_BLOCK // LANES)], axis=0)
        s = s + bias
        mx = jnp.max(s, axis=-1, keepdims=True)
        pe = jnp.exp2(s - mx)
        den = jnp.sum(pe, axis=-1, keepdims=True)
        o = jnp.dot(pe.astype(BF16), vp, preferred_element_type=F32) / den
        o_ref[0, :, cols] = jnp.where(first_half, o[:Q_BLOCK], o[Q_BLOCK:]).astype(BF16)


def _reversed_bias_windows(rel_bias):
    d = jnp.arange(BLOCK_DISTANCES)[:, None]
    m = jnp.arange(2 * LANES)[None, :]
    idx = jnp.clip(LANES * d + LANES - 1 - m, -REL_CLIP, REL_CLIP) + REL_CLIP
    return rel_bias.astype(F32)[:, idx].reshape(ATT_HEADS * BLOCK_DISTANCES, 2 * LANES)


def _band_attn(q, k, v, rel_bias):
    b, s, w = q.shape
    nq = s // Q_BLOCK
    wrev = _reversed_bias_windows(rel_bias) * LOG2E
    whole = pl.BlockSpec((1, s, w), lambda bi, i: (bi, 0, 0))
    return pl.pallas_call(
        _attn_kernel,
        grid=(b, nq),
        in_specs=[
            pl.BlockSpec((1, Q_BLOCK, w), lambda bi, i: (bi, i, 0)),
            whole, whole,
            pl.BlockSpec(wrev.shape, lambda bi, i: (0, 0)),
        ],
        out_specs=pl.BlockSpec((1, Q_BLOCK, w), lambda bi, i: (bi, i, 0)),
        out_shape=jax.ShapeDtypeStruct((b, s, w), BF16),
        scratch_shapes=[pltpu.VMEM((BLOCK_DISTANCES + 1, ATT_HEADS, LANES, LANES), F32)],
        compiler_params=pltpu.CompilerParams(
            dimension_semantics=("arbitrary", "arbitrary"), vmem_limit_bytes=VMEM_LIMIT),
        name="band_attn",
    )(q, k, v, wrev)


def _piece_transpose(tiles):
    lane = lax.broadcasted_iota(jnp.int32, tiles[0].shape, 1)
    t = list(tiles)
    for s in (4, 2, 1):
        keep = ((lane // SSM_GROUP) & s) == 0
        sh = s * SSM_GROUP
        nxt = list(t)
        for i in range(PIECES):
            if i & s == 0:
                a, b = t[i], t[i + s]
                nxt[i] = jnp.where(keep, a, pltpu.roll(b, sh, 1))
                nxt[i + s] = jnp.where(keep, pltpu.roll(a, LANES - sh, 1), b)
        t = nxt
    return t


def _ssm_kernel(u_ref, minc_ref, krow_ref, moutc_ref, lt_ref, y_ref, st_s, toep_s, mint_s, mout_s, z_s, sp_s,
                *, nb):
    npair = mint_s.shape[0]
    rows = u_ref.shape[1]
    sl = SSM_PAIR * SSM_STATE
    gl = SSM_T * SSM_GROUP

    @pl.when(pl.program_id(1) == 0)
    def _():
        st_s[...] = jnp.zeros_like(st_s)
        mint_s[...] = jnp.zeros_like(mint_s)
        mout_s[...] = jnp.zeros_like(mout_s)
        for g in range(PIECES):
            pr, gi = divmod(g, SSM_PAIR)
            for part in range(2):
                r0 = (part * SSM_PAIR + gi) * SSM_STATE
                mint_s[pr, r0:r0 + SSM_STATE, gi * gl:(gi + 1) * gl] = minc_ref[g, part].astype(BF16)
                mout_s[pr, r0:r0 + SSM_STATE, gi * gl:(gi + 1) * gl] = moutc_ref[g, part].astype(BF16)
        lane = lax.broadcasted_iota(jnp.int32, (SSM_GROUP, gl), 1)
        for g in range(PIECES):
            k = krow_ref[g]
            for j in range(SSM_T):
                blk = jnp.where(lane >= SSM_GROUP * j, pltpu.roll(k, SSM_GROUP * j, 1), 0.0) if j else k
                toep_s[g // SSM_PAIR, g % SSM_PAIR, j * SSM_GROUP:(j + 1) * SSM_GROUP, :] = blk.astype(BF16)

    for pr in range(npair):
        z_s[pr] = lax.dot_general(u_ref[pr], mint_s[pr], (((1,), (1,)), ((), ())),
                                  preferred_element_type=F32)

    lts = [(jnp.broadcast_to(lt_ref[pr, 0:1, :], (nb, sl)), jnp.broadcast_to(lt_ref[pr, 1:2, :], (nb, sl)))
           for pr in range(npair)]

    def step(c, carry):
        r0 = pl.multiple_of(c * nb, nb)
        out = []
        for pr in range(npair):
            s_re, s_im = carry[2 * pr], carry[2 * pr + 1]
            lt_re, lt_im = lts[pr]
            sp_s[pr, pl.ds(r0, nb), 0:sl] = s_re.astype(BF16)
            sp_s[pr, pl.ds(r0, nb), sl:2 * sl] = s_im.astype(BF16)
            z = z_s[pr, pl.ds(r0, nb), :]
            out.append(lt_re * s_re - lt_im * s_im + z[:, 0:sl])
            out.append(lt_re * s_im + lt_im * s_re + z[:, sl:2 * sl])
        return tuple(out)

    init = tuple(st_s[pr, part] for pr in range(npair) for part in range(2))
    fin = lax.fori_loop(0, rows // nb, step, init)
    for pr in range(npair):
        st_s[pr, 0] = fin[2 * pr]
        st_s[pr, 1] = fin[2 * pr + 1]

    for pr in range(npair):
        y = jnp.dot(sp_s[pr], mout_s[pr], preferred_element_type=F32)
        for gi in range(SSM_PAIR):
            cols = slice(gi * gl, (gi + 1) * gl)
            yg = y[:, cols] + jnp.dot(u_ref[pr, :, cols], toep_s[pr, gi], preferred_element_type=F32)
            y_ref[pr, :, cols] = jax.nn.gelu(yg, approximate=True).astype(BF16)


def _ssm_operators(a_re, a_im, log_dt, b_re, b_im, c_re, c_im, d_skip):
    hp = lax.Precision.HIGHEST
    g, p = a_re.shape
    hg = b_re.shape[-1]
    t_len = SSM_T
    g2 = g // SSM_PAIR
    dt = jnp.exp(log_dt.astype(F32))[:, None]
    are, aim = a_re.astype(F32) * dt, a_im.astype(F32) * dt
    tt = jnp.arange(t_len + 1, dtype=F32)
    mag = jnp.exp(are[:, :, None] * tt)
    pw_re, pw_im = mag * jnp.cos(aim[:, :, None] * tt), mag * jnp.sin(aim[:, :, None] * tt)
    lb_re, lb_im = pw_re[:, :, 1], pw_im[:, :, 1]
    lr, li = a_re.astype(F32), a_im.astype(F32)
    den = lr * lr + li * li
    f_re = ((lb_re - 1.0) * lr + lb_im * li) / den
    f_im = (lb_im * lr - (lb_re - 1.0) * li) / den
    bb_re = f_re[..., None] * b_re - f_im[..., None] * b_im
    bb_im = f_re[..., None] * b_im + f_im[..., None] * b_re
    crt = jnp.transpose(c_re.astype(F32), (0, 2, 1))
    cit = jnp.transpose(c_im.astype(F32), (0, 2, 1))
    idx = jnp.arange(t_len * hg)
    rep = (idx[None, :] // hg == jnp.arange(t_len)[:, None]).astype(F32)
    til = (idx[None, :] % hg == jnp.arange(hg)[:, None]).astype(F32)
    over_steps = lambda v: jnp.einsum('gpt,tx->gpx', v, rep, precision=hp)
    over_chans = lambda v: jnp.einsum('gph,hx->gpx', v, til, precision=hp)
    c_x = over_chans(crt), over_chans(cit)
    b_x = over_chans(bb_re), over_chans(bb_im)

    def c_times_powers(lo):
        w_re, w_im = over_steps(pw_re[:, :, lo:lo + t_len]), over_steps(pw_im[:, :, lo:lo + t_len])
        return c_x[0] * w_re - c_x[1] * w_im, c_x[0] * w_im + c_x[1] * w_re

    cp_re, cp_im = c_times_powers(0)
    krow = jnp.einsum('gqx,gqb->gbx', jnp.concatenate([cp_re, -cp_im], axis=1),
                      jnp.concatenate([bb_re, bb_im], axis=1), precision=hp)
    lag0_diag = ((idx[None, :] // hg == 0) & (idx[None, :] % hg == jnp.arange(hg)[:, None])).astype(F32)
    krow = krow + lag0_diag[None] * jnp.einsum('ga,ax->gx', d_skip.astype(F32), til, precision=hp)[:, None, :]

    rv_re, rv_im = over_steps(pw_re[:, :, :t_len][:, :, ::-1]), over_steps(pw_im[:, :, :t_len][:, :, ::-1])
    m_in_t = jnp.stack([rv_re * b_x[0] - rv_im * b_x[1], rv_re * b_x[1] + rv_im * b_x[0]], axis=1)
    cn_re, cn_im = c_times_powers(1)
    m_out = jnp.stack([cn_re, -cn_im], axis=1)
    lt = jnp.stack([pw_re[:, :, t_len], pw_im[:, :, t_len]], axis=0)
    lt = jnp.transpose(lt.reshape(2, g2, SSM_PAIR * p), (1, 0, 2))
    return m_in_t, krow, m_out, lt


def _ssm(uf, b, s, ops):
    m_in_t, krow, m_out, lt = ops
    nc = s // SSM_T
    npair = PIECES // SSM_PAIR
    gl = SSM_T * SSM_GROUP
    wl = SSM_PAIR * gl
    sl2 = 2 * SSM_PAIR * SSM_STATE
    cb = _largest_divisor(nc, SSM_ROWS // b)
    rows = cb * b
    per = lambda n, *shape: pl.BlockSpec((n,) + shape, lambda bd, cr: (bd,) + (0,) * len(shape))
    io = pl.BlockSpec((npair, rows, wl), lambda bd, cr: (bd, cr, 0))
    return pl.pallas_call(
        functools.partial(_ssm_kernel, nb=b),
        grid=(uf.shape[0] // npair, nc // cb),
        in_specs=[io, per(PIECES, 2, SSM_STATE, gl), per(PIECES, SSM_GROUP, gl), per(PIECES, 2, SSM_STATE, gl),
                  per(npair, 2, sl2 // 2)],
        out_specs=io,
        out_shape=jax.ShapeDtypeStruct(uf.shape, BF16),
        scratch_shapes=[pltpu.VMEM((npair, 2, b, sl2 // 2), F32), pltpu.VMEM((npair, SSM_PAIR, gl, gl), BF16),
                        pltpu.VMEM((npair, sl2, wl), BF16), pltpu.VMEM((npair, sl2, wl), BF16),
                        pltpu.VMEM((npair, rows, sl2), F32), pltpu.VMEM((npair, rows, sl2), BF16)],
        compiler_params=pltpu.CompilerParams(
            dimension_semantics=("arbitrary", "arbitrary"), vmem_limit_bytes=VMEM_LIMIT),
        name="ssm_chunk",
    )(uf, m_in_t, krow, m_out, lt)


def _out_ffn_kernel(x_ref, att_ref, ys_ref, gate_ref, wa_ref, wg_ref, wb_ref, wo_ref,
                    gf_ref, w1_ref, w2_ref, gl_ref, o_ref, ys_s):
    nb, ts, d = x_ref.shape
    rows = nb * ts
    dot = lambda a, w: jnp.dot(a, w, preferred_element_type=F32)
    nbd = ys_s.shape[0]
    pitch = ys_s.shape[1] // nb
    gl = SSM_T * SSM_GROUP
    for bd in range(nbd):
        for jo in range(SSM_T // PIECES):
            tiles = []
            for g in range(PIECES):
                lo = (g % SSM_PAIR) * gl + jo * LANES
                tiles.append(ys_ref[bd * (PIECES // SSM_PAIR) + g // SSM_PAIR, :, lo:lo + LANES].astype(F32))
            for j8, tile in enumerate(_piece_transpose(tiles)):
                for c in range(ts // SSM_T):
                    t = c * SSM_T + jo * PIECES + j8
                    ys_s[bd, pl.ds(t, nb, stride=pitch), :] = tile[c * nb:(c + 1) * nb, :]
    ys = jnp.concatenate(
        [jnp.concatenate([ys_s[bd, bi * pitch:bi * pitch + ts, :] for bi in range(nb)], axis=0)
         for bd in range(nbd)], axis=1).astype(BF16)
    y_a = dot(att_ref[...].reshape(rows, att_ref.shape[2]), wa_ref[...])
    glu = dot(ys, wg_ref[...])
    sw = glu.shape[1] // 2
    y_b = dot((glu[:, :sw] * jax.nn.sigmoid(glu[:, sw:])).astype(BF16), wb_ref[...])
    gates = gate_ref[...].reshape(rows, 2 * d)
    mixed = gates[:, :d].astype(F32) * y_a + gates[:, d:].astype(F32) * y_b
    h = x_ref[...].reshape(rows, d) + dot(mixed.astype(BF16), wo_ref[...])
    n = _rms(h, gf_ref[...]).astype(BF16)
    dff = w1_ref.shape[1]
    fc = min(dff, 1024)
    acc = h
    for lo in range(0, dff, fc):
        f = jnp.maximum(dot(n, w1_ref[:, lo:lo + fc]), 0.0)
        acc = acc + dot((f * f).astype(BF16), w2_ref[lo:lo + fc, :])
    o_ref[...] = _rms(acc, gl_ref[...]).reshape(nb, ts, d)


def _out_ffn(x, att, ys, gates, w_a, w_glu, w_b, w_out, norm_ffn, w1, w2, norm_final, ts):
    b, s, d = x.shape
    npairs, _, pair_lanes = ys.shape
    nbd = npairs * SSM_PAIR // PIECES
    row = lambda w: pl.BlockSpec((b, ts, w), lambda i: (0, i, 0))
    weights = [w_a, w_glu, w_b, w_out]
    bf = [w.astype(BF16) for w in weights]
    w1b, w2b = w1.astype(BF16), w2.astype(BF16)
    return pl.pallas_call(
        _out_ffn_kernel,
        grid=(s // ts,),
        in_specs=[row(d), row(att.shape[2]),
                  pl.BlockSpec((npairs, (ts // SSM_T) * b, pair_lanes), lambda i: (0, i, 0)),
                  row(gates.shape[2])]
        + [_const_spec(w.shape) for w in bf]
        + [_const_spec((1, d)), _const_spec(w1b.shape), _const_spec(w2b.shape), _const_spec((1, d))],
        out_specs=row(d),
        out_shape=jax.ShapeDtypeStruct((b, s, d), F32),
        scratch_shapes=[pltpu.VMEM((nbd, b * _row_pitch(ts), LANES), F32)],
        compiler_params=pltpu.CompilerParams(
            dimension_semantics=("arbitrary",), vmem_limit_bytes=VMEM_LIMIT),
        name="out_ffn",
    )(x, att, ys, gates, *bf, norm_ffn.reshape(1, d), w1b, w2b, norm_final.reshape(1, d))


def _time_block(b, s, rows):
    ts = _largest_divisor(s // SSM_T, max(1, rows // (b * SSM_T))) * SSM_T
    assert ts % 16 == 0, "bf16 blocks need whole 16-row sublane tiles"
    return ts


def kernel(x, norm_mix, w_in, b_gate, rel_bias, ssm_a_re, ssm_a_im, ssm_log_dt, ssm_b_re, ssm_b_im,
           ssm_c_re, ssm_c_im, ssm_d, w_glu, w_proj_a, w_proj_b, w_out, norm_ffn, w_ff1, w_ff2,
           norm_final):
    b, s, d = x.shape
    assert norm_mix.shape[0] == 1, "single-layer block"
    assert s % Q_BLOCK == 0 and s >= KEY_WINDOW and s % SSM_T == 0
    assert b % 16 == 0, "batch rows fill whole bf16 sublane tiles in the SSM state scan"
    q, k, v, us, gates = _in_proj(x, norm_mix[0], w_in[0], b_gate[0], _time_block(b, s, IN_PROJ_ROWS))
    att = _band_attn(q, k, v, rel_bias[0])
    ops = _ssm_operators(ssm_a_re[0], ssm_a_im[0], ssm_log_dt[0], ssm_b_re[0], ssm_b_im[0],
                         ssm_c_re[0], ssm_c_im[0], ssm_d[0])
    ys = _ssm(us, b, s, ops)
    return _out_ffn(x, att, ys, gates, w_proj_a[0], w_glu[0], w_proj_b[0], w_out[0],
                    norm_ffn[0], w_ff1[0], w_ff2[0], norm_final, _time_block(b, s, OUT_FFN_ROWS))
```

```python
import functools
import math

import jax
import jax.numpy as jnp
from jax import lax
from jax.experimental import pallas as pl
from jax.experimental.pallas import tpu as pltpu

F32 = jnp.float32
BF16 = jnp.bfloat16

CHUNK = 64
N_LEFT_CHUNKS = 8
ATT_HEADS = 8
HEAD_DIM = 64
ATT_WIDTH = ATT_HEADS * HEAD_DIM
REL_CLIP = 128
SSM_GROUP = 16
SSM_STATE = 64
EPS = 1e-6
NEG_INF = -1e30

LANES = 128
Q_BLOCK = 4 * CHUNK
ATTN_SUBS = 4
KEY_WINDOW = Q_BLOCK + N_LEFT_CHUNKS * CHUNK
LEFT_BLOCKS = (N_LEFT_CHUNKS * CHUNK) // Q_BLOCK
BLOCK_DISTANCES = (N_LEFT_CHUNKS * CHUNK) // LANES + 1
SSM_T = 16
SSM_PAIR = 2
PIECES = LANES // SSM_GROUP
SSM_ROWS = 1024
IN_PROJ_ROWS = 1024
OUT_FFN_ROWS = 512
LOG2E = math.log2(math.e)
Q_SCALE = HEAD_DIM ** -0.5 * LOG2E
VMEM_LIMIT = 56 * 1024 * 1024


def _largest_divisor(n, cap):
    return max(d for d in range(1, min(n, cap) + 1) if n % d == 0)


def _row_pitch(ts):
    assert ts % 8 == 0
    return ts + 4


def _const_spec(shape):
    nd = len(shape)
    return pl.BlockSpec(shape, lambda *_: (0,) * nd, pipeline_mode=pl.Buffered(1))


def _rms(x, g):
    ms = jnp.mean(x * x, axis=-1, keepdims=True)
    return x * lax.rsqrt(ms + EPS) * g


def _in_proj_kernel(x_ref, g_ref, w_ref, bg_ref, q_ref, k_ref, v_ref, us_ref, gate_ref, us_s):
    nb, ts, d = x_ref.shape
    u = _rms(x_ref[...].reshape(nb * ts, d), g_ref[...]).astype(BF16)

    def proj(lo, hi):
        return jnp.dot(u, w_ref[:, lo:hi], preferred_element_type=F32)

    aw = ATT_WIDTH
    nbd = us_s.shape[0]
    pitch = us_s.shape[1] // nb
    gl = SSM_T * SSM_GROUP
    g0 = 3 * aw + nbd * LANES
    half = gate_ref.shape[2] // 2

    def gate(lo):
        z = proj(g0 + lo, g0 + lo + half) + bg_ref[:, lo:lo + half]
        gate_ref[:, :, lo:lo + half] = jax.nn.sigmoid(z).astype(BF16).reshape(nb, ts, half)

    gate(0)
    us = proj(3 * aw, 3 * aw + nbd * LANES)
    for bd in range(nbd):
        for bi in range(nb):
            us_s[bd, bi * pitch:bi * pitch + ts, :] = us[bi * ts:(bi + 1) * ts, bd * LANES:(bd + 1) * LANES]
    gate(half)
    for bd in range(nbd):
        for jo in range(SSM_T // PIECES):
            tiles = [jnp.concatenate([us_s[bd, pl.ds(c * SSM_T + jo * PIECES + j8, nb, stride=pitch), :]
                                      for c in range(ts // SSM_T)], axis=0) for j8 in range(PIECES)]
            for g, tile in enumerate(_piece_transpose(tiles)):
                lo = (g % SSM_PAIR) * gl + jo * LANES
                us_ref[bd * (PIECES // SSM_PAIR) + g // SSM_PAIR, :, lo:lo + LANES] = tile.astype(BF16)
    q_ref[...] = (proj(0, aw) * Q_SCALE).astype(BF16).reshape(nb, ts, aw)
    k_ref[...] = proj(aw, 2 * aw).astype(BF16).reshape(nb, ts, aw)
    v_ref[...] = proj(2 * aw, 3 * aw).astype(BF16).reshape(nb, ts, aw)


def _in_proj(x, norm_g, w_in, b_gate, ts):
    b, s, d = x.shape
    n_in = w_in.shape[1]
    gw = b_gate.shape[0]
    sw = n_in - 3 * ATT_WIDTH - gw
    nbd = sw // LANES
    row = lambda w: pl.BlockSpec((b, ts, w), lambda i: (0, i, 0))
    ssm_rows = (ts // SSM_T) * b
    npairs = sw // (SSM_PAIR * SSM_GROUP)
    pair_lanes = SSM_PAIR * SSM_T * SSM_GROUP
    return pl.pallas_call(
        _in_proj_kernel,
        grid=(s // ts,),
        in_specs=[row(d), _const_spec((1, d)), _const_spec((d, n_in)), _const_spec((1, gw))],
        out_specs=[row(ATT_WIDTH), row(ATT_WIDTH), row(ATT_WIDTH),
                   pl.BlockSpec((npairs, ssm_rows, pair_lanes), lambda i: (0, i, 0)), row(gw)],
        out_shape=[jax.ShapeDtypeStruct((b, s, ATT_WIDTH), BF16)] * 3
        + [jax.ShapeDtypeStruct((npairs, (s // SSM_T) * b, pair_lanes), BF16),
           jax.ShapeDtypeStruct((b, s, gw), BF16)],
        scratch_shapes=[pltpu.VMEM((nbd, b * _row_pitch(ts), LANES), F32)],
        compiler_params=pltpu.CompilerParams(
            dimension_semantics=("arbitrary",), vmem_limit_bytes=VMEM_LIMIT),
        name="in_proj",
    )(x, norm_g.reshape(1, d), w_in.astype(BF16), b_gate.reshape(1, gw))


def _build_bias_tiles(wrev_ref, tiles_s):
    qi = lax.broadcasted_iota(jnp.int32, (LANES, LANES), 0)
    ri = lax.broadcasted_iota(jnp.int32, (LANES, LANES), 1)
    below = ri <= qi
    dchunk = qi // CHUNK - ri // CHUNK
    per_block = LANES // CHUNK
    for h in range(ATT_HEADS):
        for d in range(BLOCK_DISTANCES):
            row = wrev_ref[pl.ds(h * BLOCK_DISTANCES + d, 1), :]
            lo = pltpu.roll(jnp.broadcast_to(row[:, :LANES], (LANES, LANES)), 1, 1, stride=1, stride_axis=0)
            hi = pltpu.roll(jnp.broadcast_to(row[:, LANES:], (LANES, LANES)), 1, 1, stride=1, stride_axis=0)
            dc = per_block * d + dchunk
            valid = (dc >= 0) & (dc <= N_LEFT_CHUNKS)
            tiles_s[d, h] = jnp.where(valid, jnp.where(below, lo, hi), NEG_INF)
        tiles_s[BLOCK_DISTANCES, h] = jnp.full((LANES, LANES), NEG_INF, F32)


def _attn_kernel(q_ref, k_ref, v_ref, wrev_ref, o_ref, tiles_s):
    @pl.when((pl.program_id(0) == 0) & (pl.program_id(1) == 0))
    def _():
        _build_bias_tiles(wrev_ref, tiles_s)

    for sub in range(ATTN_SUBS):
        rows = slice(sub * Q_BLOCK, (sub + 1) * Q_BLOCK)
        _attn_block(q_ref.at[:, rows, :], k_ref, v_ref, o_ref.at[:, rows, :], tiles_s,
                    pl.program_id(1) * ATTN_SUBS + sub)


def _attn_block(q_ref, k_ref, v_ref, o_ref, tiles_s, i):
    start = pl.multiple_of(jnp.maximum(i - LEFT_BLOCKS, 0) * Q_BLOCK, Q_BLOCK)
    base = jnp.minimum(i, LEFT_BLOCKS) * (Q_BLOCK // LANES)
    first_half = lax.broadcasted_iota(jnp.int32, (Q_BLOCK, LANES), 1) < HEAD_DIM

    def tile_index(qs, kb):
        d = base + qs - kb
        return jnp.where((d >= 0) & (d < BLOCK_DISTANCES), d, BLOCK_DISTANCES)

    for p in range(ATT_HEADS // 2):
        cols = slice(p * LANES, (p + 1) * LANES)
        qp = q_ref[0, :, cols]
        zero = jnp.zeros_like(qp)
        qm = jnp.concatenate([jnp.where(first_half, qp, zero), jnp.where(first_half, zero, qp)], axis=0)
        kp = k_ref[0, pl.ds(start, KEY_WINDOW), cols]
        vp = v_ref[0, pl.ds(start, KEY_WINDOW), cols]
        s = lax.dot_general(qm, kp, (((1,), (1,)), ((), ())), preferred_element_type=F32)
        bias = jnp.concatenate(
            [jnp.concatenate([tiles_s[tile_index(qs, kb), 2 * p + e] for kb in range(KEY_WINDOW // LANES)], axis=1)
             for e in range(2) for qs in range(Q_BLOCK // LANES)], axis=0)
        s = s + bias
        mx = jnp.max(s, axis=-1, keepdims=True)
        pe = jnp.exp2(s - mx)
        den = jnp.sum(pe, axis=-1, keepdims=True)
        o = jnp.dot(pe.astype(BF16), vp, preferred_element_type=F32) / den
        o_ref[0, :, cols] = jnp.where(first_half, o[:Q_BLOCK], o[Q_BLOCK:]).astype(BF16)


def _reversed_bias_windows(rel_bias):
    d = jnp.arange(BLOCK_DISTANCES)[:, None]
    m = jnp.arange(2 * LANES)[None, :]
    idx = jnp.clip(LANES * d + LANES - 1 - m, -REL_CLIP, REL_CLIP) + REL_CLIP
    return rel_bias.astype(F32)[:, idx].reshape(ATT_HEADS * BLOCK_DISTANCES, 2 * LANES)


def _band_attn(q, k, v, rel_bias):
    b, s, w = q.shape
    step_rows = ATTN_SUBS * Q_BLOCK
    wrev = _reversed_bias_windows(rel_bias) * LOG2E
    whole = pl.BlockSpec((1, s, w), lambda bi, i: (bi, 0, 0))
    return pl.pallas_call(
        _attn_kernel,
        grid=(b, s // step_rows),
        in_specs=[
            pl.BlockSpec((1, step_rows, w), lambda bi, i: (bi, i, 0)),
            whole, whole,
            pl.BlockSpec(wrev.shape, lambda bi, i: (0, 0)),
        ],
        out_specs=pl.BlockSpec((1, step_rows, w), lambda bi, i: (bi, i, 0)),
        out_shape=jax.ShapeDtypeStruct((b, s, w), BF16),
        scratch_shapes=[pltpu.VMEM((BLOCK_DISTANCES + 1, ATT_HEADS, LANES, LANES), F32)],
        compiler_params=pltpu.CompilerParams(
            dimension_semantics=("arbitrary", "arbitrary"), vmem_limit_bytes=VMEM_LIMIT),
        name="band_attn",
    )(q, k, v, wrev)


def _piece_transpose(tiles):
    lane = lax.broadcasted_iota(jnp.int32, tiles[0].shape, 1)
    t = list(tiles)
    for s in (4, 2, 1):
        keep = ((lane // SSM_GROUP) & s) == 0
        sh = s * SSM_GROUP
        nxt = list(t)
        for i in range(PIECES):
            if i & s == 0:
                a, b = t[i], t[i + s]
                nxt[i] = jnp.where(keep, a, pltpu.roll(b, sh, 1))
                nxt[i + s] = jnp.where(keep, pltpu.roll(a, LANES - sh, 1), b)
        t = nxt
    return t


def _ssm_kernel(u_ref, minc_ref, krow_ref, moutc_ref, lt_ref, y_ref, st_s, toep_s, mint_s, mout_s, z_s, sp_s,
                *, nb):
    npair = mint_s.shape[0]
    rows = u_ref.shape[1]
    sl = SSM_PAIR * SSM_STATE
    gl = SSM_T * SSM_GROUP

    @pl.when(pl.program_id(1) == 0)
    def _():
        st_s[...] = jnp.zeros_like(st_s)
        mint_s[...] = jnp.zeros_like(mint_s)
        mout_s[...] = jnp.zeros_like(mout_s)
        for g in range(PIECES):
            pr, gi = divmod(g, SSM_PAIR)
            for part in range(2):
                r0 = (part * SSM_PAIR + gi) * SSM_STATE
                mint_s[pr, r0:r0 + SSM_STATE, gi * gl:(gi + 1) * gl] = minc_ref[g, part].astype(BF16)
                mout_s[pr, r0:r0 + SSM_STATE, gi * gl:(gi + 1) * gl] = moutc_ref[g, part].astype(BF16)
        lane = lax.broadcasted_iota(jnp.int32, (SSM_GROUP, gl), 1)
        for g in range(PIECES):
            k = krow_ref[g]
            for j in range(SSM_T):
                blk = jnp.where(lane >= SSM_GROUP * j, pltpu.roll(k, SSM_GROUP * j, 1), 0.0) if j else k
                toep_s[g // SSM_PAIR, g % SSM_PAIR, j * SSM_GROUP:(j + 1) * SSM_GROUP, :] = blk.astype(BF16)

    for pr in range(npair):
        z_s[pr] = lax.dot_general(u_ref[pr], mint_s[pr], (((1,), (1,)), ((), ())),
                                  preferred_element_type=F32)

    lts = [(jnp.broadcast_to(lt_ref[pr, 0:1, :], (nb, sl)), jnp.broadcast_to(lt_ref[pr, 1:2, :], (nb, sl)))
           for pr in range(npair)]

    def step(c, carry):
        r0 = pl.multiple_of(c * nb, nb)
        out = []
        for pr in range(npair):
            s_re, s_im = carry[2 * pr], carry[2 * pr + 1]
            lt_re, lt_im = lts[pr]
            sp_s[pr, pl.ds(r0, nb), 0:sl] = s_re.astype(BF16)
            sp_s[pr, pl.ds(r0, nb), sl:2 * sl] = s_im.astype(BF16)
            z = z_s[pr, pl.ds(r0, nb), :]
            out.append(lt_re * s_re - lt_im * s_im + z[:, 0:sl])
            out.append(lt_re * s_im + lt_im * s_re + z[:, sl:2 * sl])
        return tuple(out)

    init = tuple(st_s[pr, part] for pr in range(npair) for part in range(2))
    fin = lax.fori_loop(0, rows // nb, step, init)
    for pr in range(npair):
        st_s[pr, 0] = fin[2 * pr]
        st_s[pr, 1] = fin[2 * pr + 1]

    for pr in range(npair):
        y = jnp.dot(sp_s[pr], mout_s[pr], preferred_element_type=F32)
        for gi in range(SSM_PAIR):
            cols = slice(gi * gl, (gi + 1) * gl)
            yg = y[:, cols] + jnp.dot(u_ref[pr, :, cols], toep_s[pr, gi], preferred_element_type=F32)
            y_ref[pr, :, cols] = jax.nn.gelu(yg, approximate=True).astype(BF16)


def _ssm_operators(a_re, a_im, log_dt, b_re, b_im, c_re, c_im, d_skip):
    hp = lax.Precision.HIGHEST
    g, p = a_re.shape
    hg = b_re.shape[-1]
    t_len = SSM_T
    g2 = g // SSM_PAIR
    dt = jnp.exp(log_dt.astype(F32))[:, None]
    are, aim = a_re.astype(F32) * dt, a_im.astype(F32) * dt
    tt = jnp.arange(t_len + 1, dtype=F32)
    mag = jnp.exp(are[:, :, None] * tt)
    pw_re, pw_im = mag * jnp.cos(aim[:, :, None] * tt), mag * jnp.sin(aim[:, :, None] * tt)
    lb_re, lb_im = pw_re[:, :, 1], pw_im[:, :, 1]
    lr, li = a_re.astype(F32), a_im.astype(F32)
    den = lr * lr + li * li
    f_re = ((lb_re - 1.0) * lr + lb_im * li) / den
    f_im = (lb_im * lr - (lb_re - 1.0) * li) / den
    bb_re = f_re[..., None] * b_re - f_im[..., None] * b_im
    bb_im = f_re[..., None] * b_im + f_im[..., None] * b_re
    crt = jnp.transpose(c_re.astype(F32), (0, 2, 1))
    cit = jnp.transpose(c_im.astype(F32), (0, 2, 1))
    idx = jnp.arange(t_len * hg)
    rep = (idx[None, :] // hg == jnp.arange(t_len)[:, None]).astype(F32)
    til = (idx[None, :] % hg == jnp.arange(hg)[:, None]).astype(F32)
    over_steps = lambda v: jnp.einsum('gpt,tx->gpx', v, rep, precision=hp)
    over_chans = lambda v: jnp.einsum('gph,hx->gpx', v, til, precision=hp)
    c_x = over_chans(crt), over_chans(cit)
    b_x = over_chans(bb_re), over_chans(bb_im)

    def c_times_powers(lo):
        w_re, w_im = over_steps(pw_re[:, :, lo:lo + t_len]), over_steps(pw_im[:, :, lo:lo + t_len])
        return c_x[0] * w_re - c_x[1] * w_im, c_x[0] * w_im + c_x[1] * w_re

    cp_re, cp_im = c_times_powers(0)
    krow = jnp.einsum('gqx,gqb->gbx', jnp.concatenate([cp_re, -cp_im], axis=1),
                      jnp.concatenate([bb_re, bb_im], axis=1), precision=hp)
    lag0_diag = ((idx[None, :] // hg == 0) & (idx[None, :] % hg == jnp.arange(hg)[:, None])).astype(F32)
    krow = krow + lag0_diag[None] * jnp.einsum('ga,ax->gx', d_skip.astype(F32), til, precision=hp)[:, None, :]

    rv_re, rv_im = over_steps(pw_re[:, :, :t_len][:, :, ::-1]), over_steps(pw_im[:, :, :t_len][:, :, ::-1])
    m_in_t = jnp.stack([rv_re * b_x[0] - rv_im * b_x[1], rv_re * b_x[1] + rv_im * b_x[0]], axis=1)
    cn_re, cn_im = c_times_powers(1)
    m_out = jnp.stack([cn_re, -cn_im], axis=1)
    lt = jnp.stack([pw_re[:, :, t_len], pw_im[:, :, t_len]], axis=0)
    lt = jnp.transpose(lt.reshape(2, g2, SSM_PAIR * p), (1, 0, 2))
    return m_in_t, krow, m_out, lt


def _ssm(uf, b, s, ops):
    m_in_t, krow, m_out, lt = ops
    nc = s // SSM_T
    npair = PIECES // SSM_PAIR
    gl = SSM_T * SSM_GROUP
    wl = SSM_PAIR * gl
    sl2 = 2 * SSM_PAIR * SSM_STATE
    cb = _largest_divisor(nc, SSM_ROWS // b)
    rows = cb * b
    per = lambda n, *shape: pl.BlockSpec((n,) + shape, lambda bd, cr: (bd,) + (0,) * len(shape))
    io = pl.BlockSpec((npair, rows, wl), lambda bd, cr: (bd, cr, 0))
    return pl.pallas_call(
        functools.partial(_ssm_kernel, nb=b),
        grid=(uf.shape[0] // npair, nc // cb),
        in_specs=[io, per(PIECES, 2, SSM_STATE, gl), per(PIECES, SSM_GROUP, gl), per(PIECES, 2, SSM_STATE, gl),
                  per(npair, 2, sl2 // 2)],
        out_specs=io,
        out_shape=jax.ShapeDtypeStruct(uf.shape, BF16),
        scratch_shapes=[pltpu.VMEM((npair, 2, b, sl2 // 2), F32), pltpu.VMEM((npair, SSM_PAIR, gl, gl), BF16),
                        pltpu.VMEM((npair, sl2, wl), BF16), pltpu.VMEM((npair, sl2, wl), BF16),
                        pltpu.VMEM((npair, rows, sl2), F32), pltpu.VMEM((npair, rows, sl2), BF16)],
        compiler_params=pltpu.CompilerParams(
            dimension_semantics=("arbitrary", "arbitrary"), vmem_limit_bytes=VMEM_LIMIT),
        name="ssm_chunk",
    )(uf, m_in_t, krow, m_out, lt)


def _out_ffn_kernel(x_ref, att_ref, ys_ref, gate_ref, wa_ref, wg_ref, wb_ref, wo_ref,
                    gf_ref, w1_ref, w2_ref, gl_ref, o_ref, ys_s):
    nb, ts, d = x_ref.shape
    rows = nb * ts
    dot = lambda a, w: jnp.dot(a, w, preferred_element_type=F32)
    nbd = ys_s.shape[0]
    pitch = ys_s.shape[1] // nb
    gl = SSM_T * SSM_GROUP
    for bd in range(nbd):
        for jo in range(SSM_T // PIECES):
            tiles = []
            for g in range(PIECES):
                lo = (g % SSM_PAIR) * gl + jo * LANES
                tiles.append(ys_ref[bd * (PIECES // SSM_PAIR) + g // SSM_PAIR, :, lo:lo + LANES].astype(F32))
            for j8, tile in enumerate(_piece_transpose(tiles)):
                for c in range(ts // SSM_T):
                    t = c * SSM_T + jo * PIECES + j8
                    ys_s[bd, pl.ds(t, nb, stride=pitch), :] = tile[c * nb:(c + 1) * nb, :]
    ys = jnp.concatenate(
        [jnp.concatenate([ys_s[bd, bi * pitch:bi * pitch + ts, :] for bi in range(nb)], axis=0)
         for bd in range(nbd)], axis=1).astype(BF16)
    y_a = dot(att_ref[...].reshape(rows, att_ref.shape[2]), wa_ref[...])
    glu = dot(ys, wg_ref[...])
    sw = glu.shape[1] // 2
    y_b = dot((glu[:, :sw] * jax.nn.sigmoid(glu[:, sw:])).astype(BF16), wb_ref[...])
    gates = gate_ref[...].reshape(rows, 2 * d)
    mixed = gates[:, :d].astype(F32) * y_a + gates[:, d:].astype(F32) * y_b
    h = x_ref[...].reshape(rows, d) + dot(mixed.astype(BF16), wo_ref[...])
    n = _rms(h, gf_ref[...]).astype(BF16)
    dff = w1_ref.shape[1]
    fc = min(dff, 1024)
    acc = h
    for lo in range(0, dff, fc):
        f = jnp.maximum(dot(n, w1_ref[:, lo:lo + fc]), 0.0)
        acc = acc + dot((f * f).astype(BF16), w2_ref[lo:lo + fc, :])
    o_ref[...] = _rms(acc, gl_ref[...]).reshape(nb, ts, d)


def _out_ffn(x, att, ys, gates, w_a, w_glu, w_b, w_out, norm_ffn, w1, w2, norm_final, ts):
    b, s, d = x.shape
    npairs, _, pair_lanes = ys.shape
    nbd = npairs * SSM_PAIR // PIECES
    row = lambda w: pl.BlockSpec((b, ts, w), lambda i: (0, i, 0))
    weights = [w_a, w_glu, w_b, w_out]
    bf = [w.astype(BF16) for w in weights]
    w1b, w2b = w1.astype(BF16), w2.astype(BF16)
    return pl.pallas_call(
        _out_ffn_kernel,
        grid=(s // ts,),
        in_specs=[row(d), row(att.shape[2]),
                  pl.BlockSpec((npairs, (ts // SSM_T) * b, pair_lanes), lambda i: (0, i, 0)),
                  row(gates.shape[2])]
        + [_const_spec(w.shape) for w in bf]
        + [_const_spec((1, d)), _const_spec(w1b.shape), _const_spec(w2b.shape), _const_spec((1, d))],
        out_specs=row(d),
        out_shape=jax.ShapeDtypeStruct((b, s, d), F32),
        scratch_shapes=[pltpu.VMEM((nbd, b * _row_pitch(ts), LANES), F32)],
        compiler_params=pltpu.CompilerParams(
            dimension_semantics=("arbitrary",), vmem_limit_bytes=VMEM_LIMIT),
        name="out_ffn",
    )(x, att, ys, gates, *bf, norm_ffn.reshape(1, d), w1b, w2b, norm_final.reshape(1, d))


def _time_block(b, s, rows):
    ts = _largest_divisor(s // SSM_T, max(1, rows // (b * SSM_T))) * SSM_T
    assert ts % 16 == 0, "bf16 blocks need whole 16-row sublane tiles"
    return ts


def kernel(x, norm_mix, w_in, b_gate, rel_bias, ssm_a_re, ssm_a_im, ssm_log_dt, ssm_b_re, ssm_b_im,
           ssm_c_re, ssm_c_im, ssm_d, w_glu, w_proj_a, w_proj_b, w_out, norm_ffn, w_ff1, w_ff2,
           norm_final):
    b, s, d = x.shape
    assert norm_mix.shape[0] == 1, "single-layer block"
    assert s % (ATTN_SUBS * Q_BLOCK) == 0 and s >= KEY_WINDOW and s % SSM_T == 0
    assert b % 16 == 0, "batch rows fill whole bf16 sublane tiles in the SSM state scan"
    q, k, v, us, gates = _in_proj(x, norm_mix[0], w_in[0], b_gate[0], _time_block(b, s, IN_PROJ_ROWS))
    att = _band_attn(q, k, v, rel_bias[0])
    ops = _ssm_operators(ssm_a_re[0], ssm_a_im[0], ssm_log_dt[0], ssm_b_re[0], ssm_b_im[0],
                         ssm_c_re[0], ssm_c_im[0], ssm_d[0])
    ys = _ssm(us, b, s, ops)
    return _out_ffn(x, att, ys, gates, w_proj_a[0], w_glu[0], w_proj_b[0], w_out[0],
                    norm_ffn[0], w_ff1[0], w_ff2[0], norm_final, _time_block(b, s, OUT_FFN_ROWS))
```

```python
import functools
import math

import jax
import jax.numpy as jnp
from jax import lax
from jax.experimental import pallas as pl
from jax.experimental.pallas import tpu as pltpu

F32 = jnp.float32
BF16 = jnp.bfloat16

CHUNK = 64
N_LEFT_CHUNKS = 8
ATT_HEADS = 8
HEAD_DIM = 64
ATT_WIDTH = ATT_HEADS * HEAD_DIM
REL_CLIP = 128
SSM_GROUP = 16
SSM_STATE = 64
EPS = 1e-6
NEG_INF = -1e30

LANES = 128
Q_BLOCK = 4 * CHUNK
ATTN_SUBS = 4
KEY_WINDOW = Q_BLOCK + N_LEFT_CHUNKS * CHUNK
LEFT_BLOCKS = (N_LEFT_CHUNKS * CHUNK) // Q_BLOCK
BLOCK_DISTANCES = (N_LEFT_CHUNKS * CHUNK) // LANES + 1
SSM_T = 16
SSM_PAIR = 2
PIECES = LANES // SSM_GROUP
SSM_ROWS = 1024
IN_PROJ_ROWS = 1024
OUT_FFN_ROWS = 512
LOG2E = math.log2(math.e)
Q_SCALE = HEAD_DIM ** -0.5 * LOG2E
VMEM_LIMIT = 56 * 1024 * 1024


def _largest_divisor(n, cap):
    return max(d for d in range(1, min(n, cap) + 1) if n % d == 0)


def _row_pitch(ts):
    assert ts % 8 == 0
    return ts + 4


def _const_spec(shape):
    nd = len(shape)
    return pl.BlockSpec(shape, lambda *_: (0,) * nd, pipeline_mode=pl.Buffered(1))


def _rms(x, g):
    ms = jnp.mean(x * x, axis=-1, keepdims=True)
    return x * lax.rsqrt(ms + EPS) * g


def _in_proj_kernel(x_ref, g_ref, w_ref, bg_ref, q_ref, k_ref, v_ref, us_ref, gate_ref, us_s):
    nb, ts, d = x_ref.shape
    u = _rms(x_ref[...].reshape(nb * ts, d), g_ref[...]).astype(BF16)

    def proj(lo, hi):
        return jnp.dot(u, w_ref[:, lo:hi], preferred_element_type=F32)

    aw = ATT_WIDTH
    nbd = us_s.shape[0]
    pitch = us_s.shape[1] // nb
    gl = SSM_T * SSM_GROUP
    g0 = 3 * aw + nbd * LANES
    half = gate_ref.shape[2] // 2

    def gate(lo):
        z = proj(g0 + lo, g0 + lo + half) + bg_ref[:, lo:lo + half]
        gate_ref[:, :, lo:lo + half] = jax.nn.sigmoid(z).astype(BF16).reshape(nb, ts, half)

    gate(0)
    us = proj(3 * aw, 3 * aw + nbd * LANES)
    for bd in range(nbd):
        for bi in range(nb):
            us_s[bd, bi * pitch:bi * pitch + ts, :] = us[bi * ts:(bi + 1) * ts, bd * LANES:(bd + 1) * LANES]
    gate(half)
    for bd in range(nbd):
        for jo in range(SSM_T // PIECES):
            tiles = [jnp.concatenate([us_s[bd, pl.ds(c * SSM_T + jo * PIECES + j8, nb, stride=pitch), :]
                                      for c in range(ts // SSM_T)], axis=0) for j8 in range(PIECES)]
            for g, tile in enumerate(_piece_transpose(tiles)):
                lo = (g % SSM_PAIR) * gl + jo * LANES
                us_ref[bd * (PIECES // SSM_PAIR) + g // SSM_PAIR, :, lo:lo + LANES] = tile.astype(BF16)
    q_ref[...] = (proj(0, aw) * Q_SCALE).astype(BF16).reshape(nb, ts, aw)
    k_ref[...] = proj(aw, 2 * aw).astype(BF16).reshape(nb, ts, aw)
    v_ref[...] = proj(2 * aw, 3 * aw).astype(BF16).reshape(nb, ts, aw)


def _in_proj(x, norm_g, w_in, b_gate, ts):
    b, s, d = x.shape
    n_in = w_in.shape[1]
    gw = b_gate.shape[0]
    sw = n_in - 3 * ATT_WIDTH - gw
    nbd = sw // LANES
    row = lambda w: pl.BlockSpec((b, ts, w), lambda i: (0, i, 0))
    ssm_rows = (ts // SSM_T) * b
    npairs = sw // (SSM_PAIR * SSM_GROUP)
    pair_lanes = SSM_PAIR * SSM_T * SSM_GROUP
    return pl.pallas_call(
        _in_proj_kernel,
        grid=(s // ts,),
        in_specs=[row(d), _const_spec((1, d)), _const_spec((d, n_in)), _const_spec((1, gw))],
        out_specs=[row(ATT_WIDTH), row(ATT_WIDTH), row(ATT_WIDTH),
                   pl.BlockSpec((npairs, ssm_rows, pair_lanes), lambda i: (0, i, 0)), row(gw)],
        out_shape=[jax.ShapeDtypeStruct((b, s, ATT_WIDTH), BF16)] * 3
        + [jax.ShapeDtypeStruct((npairs, (s // SSM_T) * b, pair_lanes), BF16),
           jax.ShapeDtypeStruct((b, s, gw), BF16)],
        scratch_shapes=[pltpu.VMEM((nbd, b * _row_pitch(ts), LANES), F32)],
        compiler_params=pltpu.CompilerParams(
            dimension_semantics=("arbitrary",), vmem_limit_bytes=VMEM_LIMIT),
        name="in_proj",
    )(x, norm_g.reshape(1, d), w_in.astype(BF16), b_gate.reshape(1, gw))


def _build_bias_tiles(wrev_ref, tiles_s):
    qi = lax.broadcasted_iota(jnp.int32, (LANES, LANES), 0)
    ri = lax.broadcasted_iota(jnp.int32, (LANES, LANES), 1)
    below = ri <= qi
    dchunk = qi // CHUNK - ri // CHUNK
    per_block = LANES // CHUNK
    for h in range(ATT_HEADS):
        for d in range(BLOCK_DISTANCES):
            row = wrev_ref[pl.ds(h * BLOCK_DISTANCES + d, 1), :]
            lo = pltpu.roll(jnp.broadcast_to(row[:, :LANES], (LANES, LANES)), 1, 1, stride=1, stride_axis=0)
            hi = pltpu.roll(jnp.broadcast_to(row[:, LANES:], (LANES, LANES)), 1, 1, stride=1, stride_axis=0)
            dc = per_block * d + dchunk
            valid = (dc >= 0) & (dc <= N_LEFT_CHUNKS)
            tiles_s[d, h] = jnp.where(valid, jnp.where(below, lo, hi), NEG_INF)
        tiles_s[BLOCK_DISTANCES, h] = jnp.full((LANES, LANES), NEG_INF, F32)


def _attn_kernel(q_ref, k_ref, v_ref, wrev_ref, o_ref, tiles_s):
    @pl.when((pl.program_id(0) == 0) & (pl.program_id(1) == 0))
    def _():
        _build_bias_tiles(wrev_ref, tiles_s)

    for sub in range(ATTN_SUBS):
        rows = slice(sub * Q_BLOCK, (sub + 1) * Q_BLOCK)
        _attn_block(q_ref.at[:, rows, :], k_ref, v_ref, o_ref.at[:, rows, :], tiles_s,
                    pl.program_id(1) * ATTN_SUBS + sub)


def _attn_block(q_ref, k_ref, v_ref, o_ref, tiles_s, i):
    start = pl.multiple_of(jnp.maximum(i - LEFT_BLOCKS, 0) * Q_BLOCK, Q_BLOCK)
    base = jnp.minimum(i, LEFT_BLOCKS) * (Q_BLOCK // LANES)
    first_half = lax.broadcasted_iota(jnp.int32, (Q_BLOCK, LANES), 1) < HEAD_DIM

    def tile_index(qs, kb):
        d = base + qs - kb
        return jnp.where((d >= 0) & (d < BLOCK_DISTANCES), d, BLOCK_DISTANCES)

    for p in range(ATT_HEADS // 2):
        cols = slice(p * LANES, (p + 1) * LANES)
        qp = q_ref[0, :, cols]
        zero = jnp.zeros_like(qp)
        qm = jnp.concatenate([jnp.where(first_half, qp, zero), jnp.where(first_half, zero, qp)], axis=0)
        kp = k_ref[0, pl.ds(start, KEY_WINDOW), cols]
        vp = v_ref[0, pl.ds(start, KEY_WINDOW), cols]
        s = lax.dot_general(qm, kp, (((1,), (1,)), ((), ())), preferred_element_type=F32)
        bias = jnp.concatenate(
            [jnp.concatenate([tiles_s[tile_index(qs, kb), 2 * p + e] for kb in range(KEY_WINDOW // LANES)], axis=1)
             for e in range(2) for qs in range(Q_BLOCK // LANES)], axis=0)
        s = s + bias
        mx = jnp.max(s, axis=-1, keepdims=True)
        pe = jnp.exp2((s - mx).astype(BF16))
        oa = jnp.dot(pe, jnp.concatenate([vp, jnp.ones_like(vp)], axis=1), preferred_element_type=F32)
        o = oa[:, :LANES] / oa[:, LANES:]
        o_ref[0, :, cols] = jnp.where(first_half, o[:Q_BLOCK], o[Q_BLOCK:]).astype(BF16)


def _reversed_bias_windows(rel_bias):
    d = jnp.arange(BLOCK_DISTANCES)[:, None]
    m = jnp.arange(2 * LANES)[None, :]
    idx = jnp.clip(LANES * d + LANES - 1 - m, -REL_CLIP, REL_CLIP) + REL_CLIP
    return rel_bias.astype(F32)[:, idx].reshape(ATT_HEADS * BLOCK_DISTANCES, 2 * LANES)


def _band_attn(q, k, v, rel_bias):
    b, s, w = q.shape
    step_rows = ATTN_SUBS * Q_BLOCK
    wrev = _reversed_bias_windows(rel_bias) * LOG2E
    whole = pl.BlockSpec((1, s, w), lambda bi, i: (bi, 0, 0))
    return pl.pallas_call(
        _attn_kernel,
        grid=(b, s // step_rows),
        in_specs=[
            pl.BlockSpec((1, step_rows, w), lambda bi, i: (bi, i, 0)),
            whole, whole,
            pl.BlockSpec(wrev.shape, lambda bi, i: (0, 0)),
        ],
        out_specs=pl.BlockSpec((1, step_rows, w), lambda bi, i: (bi, i, 0)),
        out_shape=jax.ShapeDtypeStruct((b, s, w), BF16),
        scratch_shapes=[pltpu.VMEM((BLOCK_DISTANCES + 1, ATT_HEADS, LANES, LANES), F32)],
        compiler_params=pltpu.CompilerParams(
            dimension_semantics=("arbitrary", "arbitrary"), vmem_limit_bytes=VMEM_LIMIT),
        name="band_attn",
    )(q, k, v, wrev)


def _piece_transpose(tiles):
    lane = lax.broadcasted_iota(jnp.int32, tiles[0].shape, 1)
    t = list(tiles)
    for s in (4, 2, 1):
        keep = ((lane // SSM_GROUP) & s) == 0
        sh = s * SSM_GROUP
        nxt = list(t)
        for i in range(PIECES):
            if i & s == 0:
                a, b = t[i], t[i + s]
                nxt[i] = jnp.where(keep, a, pltpu.roll(b, sh, 1))
                nxt[i + s] = jnp.where(keep, pltpu.roll(a, LANES - sh, 1), b)
        t = nxt
    return t


def _ssm_kernel(u_ref, minc_ref, krow_ref, moutc_ref, lt_ref, y_ref, st_s, toep_s, mint_s, mout_s, z_s, sp_s,
                *, nb):
    npair = mint_s.shape[0]
    rows = u_ref.shape[1]
    sl = SSM_PAIR * SSM_STATE
    gl = SSM_T * SSM_GROUP

    @pl.when(pl.program_id(1) == 0)
    def _():
        st_s[...] = jnp.zeros_like(st_s)
        mint_s[...] = jnp.zeros_like(mint_s)
        mout_s[...] = jnp.zeros_like(mout_s)
        for g in range(PIECES):
            pr, gi = divmod(g, SSM_PAIR)
            for part in range(2):
                r0 = (part * SSM_PAIR + gi) * SSM_STATE
                mint_s[pr, r0:r0 + SSM_STATE, gi * gl:(gi + 1) * gl] = minc_ref[g, part].astype(BF16)
                mout_s[pr, r0:r0 + SSM_STATE, gi * gl:(gi + 1) * gl] = moutc_ref[g, part].astype(BF16)
        lane = lax.broadcasted_iota(jnp.int32, (SSM_GROUP, gl), 1)
        for g in range(PIECES):
            k = krow_ref[g]
            for j in range(SSM_T):
                blk = jnp.where(lane >= SSM_GROUP * j, pltpu.roll(k, SSM_GROUP * j, 1), 0.0) if j else k
                toep_s[g // SSM_PAIR, g % SSM_PAIR, j * SSM_GROUP:(j + 1) * SSM_GROUP, :] = blk.astype(BF16)

    for pr in range(npair):
        z_s[pr] = lax.dot_general(u_ref[pr], mint_s[pr], (((1,), (1,)), ((), ())),
                                  preferred_element_type=F32)

    lts = [(jnp.broadcast_to(lt_ref[pr, 0:1, :], (nb, sl)), jnp.broadcast_to(lt_ref[pr, 1:2, :], (nb, sl)))
           for pr in range(npair)]

    def step(c, carry):
        r0 = pl.multiple_of(c * nb, nb)
        out = []
        for pr in range(npair):
            s_re, s_im = carry[2 * pr], carry[2 * pr + 1]
            lt_re, lt_im = lts[pr]
            sp_s[pr, pl.ds(r0, nb), 0:sl] = s_re.astype(BF16)
            sp_s[pr, pl.ds(r0, nb), sl:2 * sl] = s_im.astype(BF16)
            z = z_s[pr, pl.ds(r0, nb), :]
            out.append(lt_re * s_re - lt_im * s_im + z[:, 0:sl])
            out.append(lt_re * s_im + lt_im * s_re + z[:, sl:2 * sl])
        return tuple(out)

    init = tuple(st_s[pr, part] for pr in range(npair) for part in range(2))
    fin = lax.fori_loop(0, rows // nb, step, init)
    for pr in range(npair):
        st_s[pr, 0] = fin[2 * pr]
        st_s[pr, 1] = fin[2 * pr + 1]

    for pr in range(npair):
        y = jnp.dot(sp_s[pr], mout_s[pr], preferred_element_type=F32)
        for gi in range(SSM_PAIR):
            cols = slice(gi * gl, (gi + 1) * gl)
            yg = y[:, cols] + jnp.dot(u_ref[pr, :, cols], toep_s[pr, gi], preferred_element_type=F32)
            y_ref[pr, :, cols] = jax.nn.gelu(yg, approximate=True).astype(BF16)


def _ssm_operators(a_re, a_im, log_dt, b_re, b_im, c_re, c_im, d_skip):
    hp = lax.Precision.HIGHEST
    g, p = a_re.shape
    hg = b_re.shape[-1]
    t_len = SSM_T
    g2 = g // SSM_PAIR
    dt = jnp.exp(log_dt.astype(F32))[:, None]
    are, aim = a_re.astype(F32) * dt, a_im.astype(F32) * dt
    tt = jnp.arange(t_len + 1, dtype=F32)
    mag = jnp.exp(are[:, :, None] * tt)
    pw_re, pw_im = mag * jnp.cos(aim[:, :, None] * tt), mag * jnp.sin(aim[:, :, None] * tt)
    lb_re, lb_im = pw_re[:, :, 1], pw_im[:, :, 1]
    lr, li = a_re.astype(F32), a_im.astype(F32)
    den = lr * lr + li * li
    f_re = ((lb_re - 1.0) * lr + lb_im * li) / den
    f_im = (lb_im * lr - (lb_re - 1.0) * li) / den
    bb_re = f_re[..., None] * b_re - f_im[..., None] * b_im
    bb_im = f_re[..., None] * b_im + f_im[..., None] * b_re
    crt = jnp.transpose(c_re.astype(F32), (0, 2, 1))
    cit = jnp.transpose(c_im.astype(F32), (0, 2, 1))
    idx = jnp.arange(t_len * hg)
    rep = (idx[None, :] // hg == jnp.arange(t_len)[:, None]).astype(F32)
    til = (idx[None, :] % hg == jnp.arange(hg)[:, None]).astype(F32)
    over_steps = lambda v: jnp.einsum('gpt,tx->gpx', v, rep, precision=hp)
    over_chans = lambda v: jnp.einsum('gph,hx->gpx', v, til, precision=hp)
    c_x = over_chans(crt), over_chans(cit)
    b_x = over_chans(bb_re), over_chans(bb_im)

    def c_times_powers(lo):
        w_re, w_im = over_steps(pw_re[:, :, lo:lo + t_len]), over_steps(pw_im[:, :, lo:lo + t_len])
        return c_x[0] * w_re - c_x[1] * w_im, c_x[0] * w_im + c_x[1] * w_re

    cp_re, cp_im = c_times_powers(0)
    krow = jnp.einsum('gqx,gqb->gbx', jnp.concatenate([cp_re, -cp_im], axis=1),
                      jnp.concatenate([bb_re, bb_im], axis=1), precision=hp)
    lag0_diag = ((idx[None, :] // hg == 0) & (idx[None, :] % hg == jnp.arange(hg)[:, None])).astype(F32)
    krow = krow + lag0_diag[None] * jnp.einsum('ga,ax->gx', d_skip.astype(F32), til, precision=hp)[:, None, :]

    rv_re, rv_im = over_steps(pw_re[:, :, :t_len][:, :, ::-1]), over_steps(pw_im[:, :, :t_len][:, :, ::-1])
    m_in_t = jnp.stack([rv_re * b_x[0] - rv_im * b_x[1], rv_re * b_x[1] + rv_im * b_x[0]], axis=1)
    cn_re, cn_im = c_times_powers(1)
    m_out = jnp.stack([cn_re, -cn_im], axis=1)
    lt = jnp.stack([pw_re[:, :, t_len], pw_im[:, :, t_len]], axis=0)
    lt = jnp.transpose(lt.reshape(2, g2, SSM_PAIR * p), (1, 0, 2))
    return m_in_t, krow, m_out, lt


def _ssm(uf, b, s, ops):
    m_in_t, krow, m_out, lt = ops
    nc = s // SSM_T
    npair = PIECES // SSM_PAIR
    gl = SSM_T * SSM_GROUP
    wl = SSM_PAIR * gl
    sl2 = 2 * SSM_PAIR * SSM_STATE
    cb = _largest_divisor(nc, SSM_ROWS // b)
    rows = cb * b
    per = lambda n, *shape: pl.BlockSpec((n,) + shape, lambda bd, cr: (bd,) + (0,) * len(shape))
    io = pl.BlockSpec((npair, rows, wl), lambda bd, cr: (bd, cr, 0))
    return pl.pallas_call(
        functools.partial(_ssm_kernel, nb=b),
        grid=(uf.shape[0] // npair, nc // cb),
        in_specs=[io, per(PIECES, 2, SSM_STATE, gl), per(PIECES, SSM_GROUP, gl), per(PIECES, 2, SSM_STATE, gl),
                  per(npair, 2, sl2 // 2)],
        out_specs=io,
        out_shape=jax.ShapeDtypeStruct(uf.shape, BF16),
        scratch_shapes=[pltpu.VMEM((npair, 2, b, sl2 // 2), F32), pltpu.VMEM((npair, SSM_PAIR, gl, gl), BF16),
                        pltpu.VMEM((npair, sl2, wl), BF16), pltpu.VMEM((npair, sl2, wl), BF16),
                        pltpu.VMEM((npair, rows, sl2), F32), pltpu.VMEM((npair, rows, sl2), BF16)],
        compiler_params=pltpu.CompilerParams(
            dimension_semantics=("arbitrary", "arbitrary"), vmem_limit_bytes=VMEM_LIMIT),
        name="ssm_chunk",
    )(uf, m_in_t, krow, m_out, lt)


def _out_ffn_kernel(x_ref, att_ref, ys_ref, gate_ref, wa_ref, wg_ref, wb_ref, wo_ref,
                    gf_ref, w1_ref, w2_ref, gl_ref, o_ref, ys_s):
    nb, ts, d = x_ref.shape
    rows = nb * ts
    dot = lambda a, w: jnp.dot(a, w, preferred_element_type=F32)
    nbd = ys_s.shape[0]
    pitch = ys_s.shape[1] // nb
    gl = SSM_T * SSM_GROUP
    for bd in range(nbd):
        for jo in range(SSM_T // PIECES):
            tiles = []
            for g in range(PIECES):
                lo = (g % SSM_PAIR) * gl + jo * LANES
                tiles.append(ys_ref[bd * (PIECES // SSM_PAIR) + g // SSM_PAIR, :, lo:lo + LANES].astype(F32))
            for j8, tile in enumerate(_piece_transpose(tiles)):
                for c in range(ts // SSM_T):
                    t = c * SSM_T + jo * PIECES + j8
                    ys_s[bd, pl.ds(t, nb, stride=pitch), :] = tile[c * nb:(c + 1) * nb, :]
    ys = jnp.concatenate(
        [jnp.concatenate([ys_s[bd, bi * pitch:bi * pitch + ts, :] for bi in range(nb)], axis=0)
         for bd in range(nbd)], axis=1).astype(BF16)
    y_a = dot(att_ref[...].reshape(rows, att_ref.shape[2]), wa_ref[...])
    glu = dot(ys, wg_ref[...])
    sw = glu.shape[1] // 2
    y_b = dot((glu[:, :sw] * jax.nn.sigmoid(glu[:, sw:])).astype(BF16), wb_ref[...])
    gates = gate_ref[...].reshape(rows, 2 * d)
    mixed = gates[:, :d].astype(F32) * y_a + gates[:, d:].astype(F32) * y_b
    h = x_ref[...].reshape(rows, d) + dot(mixed.astype(BF16), wo_ref[...])
    n = _rms(h, gf_ref[...]).astype(BF16)
    dff = w1_ref.shape[1]
    fc = min(dff, 1024)
    acc = h
    for lo in range(0, dff, fc):
        f = jnp.maximum(dot(n, w1_ref[:, lo:lo + fc]), 0.0)
        acc = acc + dot((f * f).astype(BF16), w2_ref[lo:lo + fc, :])
    o_ref[...] = _rms(acc, gl_ref[...]).reshape(nb, ts, d)


def _out_ffn(x, att, ys, gates, w_a, w_glu, w_b, w_out, norm_ffn, w1, w2, norm_final, ts):
    b, s, d = x.shape
    npairs, _, pair_lanes = ys.shape
    nbd = npairs * SSM_PAIR // PIECES
    row = lambda w: pl.BlockSpec((b, ts, w), lambda i: (0, i, 0))
    weights = [w_a, w_glu, w_b, w_out]
    bf = [w.astype(BF16) for w in weights]
    w1b, w2b = w1.astype(BF16), w2.astype(BF16)
    return pl.pallas_call(
        _out_ffn_kernel,
        grid=(s // ts,),
        in_specs=[row(d), row(att.shape[2]),
                  pl.BlockSpec((npairs, (ts // SSM_T) * b, pair_lanes), lambda i: (0, i, 0)),
                  row(gates.shape[2])]
        + [_const_spec(w.shape) for w in bf]
        + [_const_spec((1, d)), _const_spec(w1b.shape), _const_spec(w2b.shape), _const_spec((1, d))],
        out_specs=row(d),
        out_shape=jax.ShapeDtypeStruct((b, s, d), F32),
        scratch_shapes=[pltpu.VMEM((nbd, b * _row_pitch(ts), LANES), F32)],
        compiler_params=pltpu.CompilerParams(
            dimension_semantics=("arbitrary",), vmem_limit_bytes=VMEM_LIMIT),
        name="out_ffn",
    )(x, att, ys, gates, *bf, norm_ffn.reshape(1, d), w1b, w2b, norm_final.reshape(1, d))


def _time_block(b, s, rows):
    ts = _largest_divisor(s // SSM_T, max(1, rows // (b * SSM_T))) * SSM_T
    assert ts % 16 == 0, "bf16 blocks need whole 16-row sublane tiles"
    return ts


def kernel(x, norm_mix, w_in, b_gate, rel_bias, ssm_a_re, ssm_a_im, ssm_log_dt, ssm_b_re, ssm_b_im,
           ssm_c_re, ssm_c_im, ssm_d, w_glu, w_proj_a, w_proj_b, w_out, norm_ffn, w_ff1, w_ff2,
           norm_final):
    b, s, d = x.shape
    assert norm_mix.shape[0] == 1, "single-layer block"
    assert s % (ATTN_SUBS * Q_BLOCK) == 0 and s >= KEY_WINDOW and s % SSM_T == 0
    assert b % 16 == 0, "batch rows fill whole bf16 sublane tiles in the SSM state scan"
    q, k, v, us, gates = _in_proj(x, norm_mix[0], w_in[0], b_gate[0], _time_block(b, s, IN_PROJ_ROWS))
    att = _band_attn(q, k, v, rel_bias[0])
    ops = _ssm_operators(ssm_a_re[0], ssm_a_im[0], ssm_log_dt[0], ssm_b_re[0], ssm_b_im[0],
                         ssm_c_re[0], ssm_c_im[0], ssm_d[0])
    ys = _ssm(us, b, s, ops)
    return _out_ffn(x, att, ys, gates, w_proj_a[0], w_glu[0], w_proj_b[0], w_out[0],
                    norm_ffn[0], w_ff1[0], w_ff2[0], norm_final, _time_block(b, s, OUT_FFN_ROWS))
```

```python
import functools
import math

import jax
import jax.numpy as jnp
from jax import lax
from jax.experimental import pallas as pl
from jax.experimental.pallas import tpu as pltpu

F32 = jnp.float32
BF16 = jnp.bfloat16

CHUNK = 64
N_LEFT_CHUNKS = 8
ATT_HEADS = 8
HEAD_DIM = 64
ATT_WIDTH = ATT_HEADS * HEAD_DIM
REL_CLIP = 128
SSM_GROUP = 16
SSM_STATE = 64
EPS = 1e-6
NEG_INF = -1e30

LANES = 128
Q_BLOCK = 4 * CHUNK
ATTN_SUBS = 8
KEY_WINDOW = Q_BLOCK + N_LEFT_CHUNKS * CHUNK
LEFT_BLOCKS = (N_LEFT_CHUNKS * CHUNK) // Q_BLOCK
BLOCK_DISTANCES = (N_LEFT_CHUNKS * CHUNK) // LANES + 1
SSM_T = 16
SSM_PAIR = 2
PIECES = LANES // SSM_GROUP
SSM_ROWS = 1024
IN_PROJ_ROWS = 1024
OUT_FFN_ROWS = 512
LOG2E = math.log2(math.e)
Q_SCALE = HEAD_DIM ** -0.5 * LOG2E
VMEM_LIMIT = 56 * 1024 * 1024


def _largest_divisor(n, cap):
    return max(d for d in range(1, min(n, cap) + 1) if n % d == 0)


def _row_pitch(ts):
    assert ts % 8 == 0
    return ts + 4


def _const_spec(shape):
    nd = len(shape)
    return pl.BlockSpec(shape, lambda *_: (0,) * nd, pipeline_mode=pl.Buffered(1))


def _rms(x, g):
    ms = jnp.mean(x * x, axis=-1, keepdims=True)
    return x * lax.rsqrt(ms + EPS) * g


def _in_proj_kernel(x_ref, g_ref, w_ref, bg_ref, q_ref, k_ref, v_ref, us_ref, gate_ref, us_s):
    nb, ts, d = x_ref.shape
    u = _rms(x_ref[...].reshape(nb * ts, d), g_ref[...]).astype(BF16)

    def proj(lo, hi):
        return jnp.dot(u, w_ref[:, lo:hi], preferred_element_type=F32)

    aw = ATT_WIDTH
    nbd = us_s.shape[0]
    pitch = us_s.shape[1] // nb
    gl = SSM_T * SSM_GROUP
    g0 = 3 * aw + nbd * LANES
    half = gate_ref.shape[2] // 2

    def gate(lo):
        z = proj(g0 + lo, g0 + lo + half) + bg_ref[:, lo:lo + half]
        gate_ref[:, :, lo:lo + half] = jax.nn.sigmoid(z).astype(BF16).reshape(nb, ts, half)

    gate(0)
    us = proj(3 * aw, 3 * aw + nbd * LANES)
    for bd in range(nbd):
        for bi in range(nb):
            us_s[bd, bi * pitch:bi * pitch + ts, :] = us[bi * ts:(bi + 1) * ts, bd * LANES:(bd + 1) * LANES]
    gate(half)
    for bd in range(nbd):
        for jo in range(SSM_T // PIECES):
            tiles = [jnp.concatenate([us_s[bd, pl.ds(c * SSM_T + jo * PIECES + j8, nb, stride=pitch), :]
                                      for c in range(ts // SSM_T)], axis=0) for j8 in range(PIECES)]
            for g, tile in enumerate(_piece_transpose(tiles)):
                lo = (g % SSM_PAIR) * gl + jo * LANES
                us_ref[bd * (PIECES // SSM_PAIR) + g // SSM_PAIR, :, lo:lo + LANES] = tile.astype(BF16)
    q_ref[...] = (proj(0, aw) * Q_SCALE).astype(BF16).reshape(nb, ts, aw)
    k_ref[...] = proj(aw, 2 * aw).astype(BF16).reshape(nb, ts, aw)
    v_ref[...] = proj(2 * aw, 3 * aw).astype(BF16).reshape(nb, ts, aw)


def _in_proj(x, norm_g, w_in, b_gate, ts):
    b, s, d = x.shape
    n_in = w_in.shape[1]
    gw = b_gate.shape[0]
    sw = n_in - 3 * ATT_WIDTH - gw
    nbd = sw // LANES
    row = lambda w: pl.BlockSpec((b, ts, w), lambda i: (0, i, 0))
    ssm_rows = (ts // SSM_T) * b
    npairs = sw // (SSM_PAIR * SSM_GROUP)
    pair_lanes = SSM_PAIR * SSM_T * SSM_GROUP
    return pl.pallas_call(
        _in_proj_kernel,
        grid=(s // ts,),
        in_specs=[row(d), _const_spec((1, d)), _const_spec((d, n_in)), _const_spec((1, gw))],
        out_specs=[row(ATT_WIDTH), row(ATT_WIDTH), row(ATT_WIDTH),
                   pl.BlockSpec((npairs, ssm_rows, pair_lanes), lambda i: (0, i, 0)), row(gw)],
        out_shape=[jax.ShapeDtypeStruct((b, s, ATT_WIDTH), BF16)] * 3
        + [jax.ShapeDtypeStruct((npairs, (s // SSM_T) * b, pair_lanes), BF16),
           jax.ShapeDtypeStruct((b, s, gw), BF16)],
        scratch_shapes=[pltpu.VMEM((nbd, b * _row_pitch(ts), LANES), F32)],
        compiler_params=pltpu.CompilerParams(
            dimension_semantics=("arbitrary",), vmem_limit_bytes=VMEM_LIMIT),
        name="in_proj",
    )(x, norm_g.reshape(1, d), w_in.astype(BF16), b_gate.reshape(1, gw))


def _build_bias_tiles(wrev_ref, tiles_s):
    qi = lax.broadcasted_iota(jnp.int32, (LANES, LANES), 0)
    ri = lax.broadcasted_iota(jnp.int32, (LANES, LANES), 1)
    below = ri <= qi
    dchunk = qi // CHUNK - ri // CHUNK
    per_block = LANES // CHUNK
    for h in range(ATT_HEADS):
        for d in range(BLOCK_DISTANCES):
            row = wrev_ref[pl.ds(h * BLOCK_DISTANCES + d, 1), :]
            lo = pltpu.roll(jnp.broadcast_to(row[:, :LANES], (LANES, LANES)), 1, 1, stride=1, stride_axis=0)
            hi = pltpu.roll(jnp.broadcast_to(row[:, LANES:], (LANES, LANES)), 1, 1, stride=1, stride_axis=0)
            dc = per_block * d + dchunk
            valid = (dc >= 0) & (dc <= N_LEFT_CHUNKS)
            tiles_s[d, h] = jnp.where(valid, jnp.where(below, lo, hi), NEG_INF)
        tiles_s[BLOCK_DISTANCES, h] = jnp.full((LANES, LANES), NEG_INF, F32)


def _attn_kernel(q_ref, k_ref, v_ref, wrev_ref, o_ref, tiles_s):
    @pl.when((pl.program_id(0) == 0) & (pl.program_id(1) == 0))
    def _():
        _build_bias_tiles(wrev_ref, tiles_s)

    subs = q_ref.shape[1] // Q_BLOCK
    for sub in range(subs):
        rows = slice(sub * Q_BLOCK, (sub + 1) * Q_BLOCK)
        _attn_block(q_ref.at[:, rows, :], k_ref, v_ref, o_ref.at[:, rows, :], tiles_s,
                    pl.program_id(1) * subs + sub)


def _attn_block(q_ref, k_ref, v_ref, o_ref, tiles_s, i):
    start = pl.multiple_of(jnp.maximum(i - LEFT_BLOCKS, 0) * Q_BLOCK, Q_BLOCK)
    base = jnp.minimum(i, LEFT_BLOCKS) * (Q_BLOCK // LANES)
    first_half = lax.broadcasted_iota(jnp.int32, (Q_BLOCK, LANES), 1) < HEAD_DIM

    def tile_index(qs, kb):
        d = base + qs - kb
        return jnp.where((d >= 0) & (d < BLOCK_DISTANCES), d, BLOCK_DISTANCES)

    for p in range(ATT_HEADS // 2):
        cols = slice(p * LANES, (p + 1) * LANES)
        qp = q_ref[0, :, cols]
        zero = jnp.zeros_like(qp)
        qm = jnp.concatenate([jnp.where(first_half, qp, zero), jnp.where(first_half, zero, qp)], axis=0)
        kp = k_ref[0, pl.ds(start, KEY_WINDOW), cols]
        vp = v_ref[0, pl.ds(start, KEY_WINDOW), cols]
        s = lax.dot_general(qm, kp, (((1,), (1,)), ((), ())), preferred_element_type=F32)
        bias = jnp.concatenate(
            [jnp.concatenate([tiles_s[tile_index(qs, kb), 2 * p + e] for kb in range(KEY_WINDOW // LANES)], axis=1)
             for e in range(2) for qs in range(Q_BLOCK // LANES)], axis=0)
        s = s + bias
        mx = jnp.max(s, axis=-1, keepdims=True)
        pe = jnp.exp2((s - mx).astype(BF16))
        oa = jnp.dot(pe, jnp.concatenate([vp, jnp.ones_like(vp)], axis=1), preferred_element_type=F32)
        o = oa[:, :LANES] / oa[:, LANES:]
        o_ref[0, :, cols] = jnp.where(first_half, o[:Q_BLOCK], o[Q_BLOCK:]).astype(BF16)


def _reversed_bias_windows(rel_bias):
    d = jnp.arange(BLOCK_DISTANCES)[:, None]
    m = jnp.arange(2 * LANES)[None, :]
    idx = jnp.clip(LANES * d + LANES - 1 - m, -REL_CLIP, REL_CLIP) + REL_CLIP
    return rel_bias.astype(F32)[:, idx].reshape(ATT_HEADS * BLOCK_DISTANCES, 2 * LANES)


def _band_attn(q, k, v, rel_bias):
    b, s, w = q.shape
    step_rows = _largest_divisor(s // Q_BLOCK, ATTN_SUBS) * Q_BLOCK
    wrev = _reversed_bias_windows(rel_bias) * LOG2E
    whole = pl.BlockSpec((1, s, w), lambda bi, i: (bi, 0, 0))
    return pl.pallas_call(
        _attn_kernel,
        grid=(b, s // step_rows),
        in_specs=[
            pl.BlockSpec((1, step_rows, w), lambda bi, i: (bi, i, 0)),
            whole, whole,
            pl.BlockSpec(wrev.shape, lambda bi, i: (0, 0)),
        ],
        out_specs=pl.BlockSpec((1, step_rows, w), lambda bi, i: (bi, i, 0)),
        out_shape=jax.ShapeDtypeStruct((b, s, w), BF16),
        scratch_shapes=[pltpu.VMEM((BLOCK_DISTANCES + 1, ATT_HEADS, LANES, LANES), F32)],
        compiler_params=pltpu.CompilerParams(
            dimension_semantics=("arbitrary", "arbitrary"), vmem_limit_bytes=VMEM_LIMIT),
        name="band_attn",
    )(q, k, v, wrev)


def _piece_transpose(tiles):
    lane = lax.broadcasted_iota(jnp.int32, tiles[0].shape, 1)
    t = list(tiles)
    for s in (4, 2, 1):
        keep = ((lane // SSM_GROUP) & s) == 0
        sh = s * SSM_GROUP
        nxt = list(t)
        for i in range(PIECES):
            if i & s == 0:
                a, b = t[i], t[i + s]
                nxt[i] = jnp.where(keep, a, pltpu.roll(b, sh, 1))
                nxt[i + s] = jnp.where(keep, pltpu.roll(a, LANES - sh, 1), b)
        t = nxt
    return t


def _gelu_tanh(x):
    a = -2.0 * math.sqrt(2.0 / math.pi) * LOG2E
    return x / (1.0 + jnp.exp2(x * (a + (a * 0.044715) * (x * x))))


def _ssm_kernel(u_ref, minc_ref, krow_ref, moutc_ref, lt_ref, y_ref, st_s, toep_s, mint_s, mout_s, z_s, sp_s,
                *, nb):
    npair = mint_s.shape[0]
    rows = u_ref.shape[1]
    sl = SSM_PAIR * SSM_STATE
    gl = SSM_T * SSM_GROUP

    @pl.when(pl.program_id(1) == 0)
    def _():
        st_s[...] = jnp.zeros_like(st_s)
        mint_s[...] = jnp.zeros_like(mint_s)
        mout_s[...] = jnp.zeros_like(mout_s)
        for g in range(PIECES):
            pr, gi = divmod(g, SSM_PAIR)
            for part in range(2):
                r0 = (part * SSM_PAIR + gi) * SSM_STATE
                mint_s[pr, r0:r0 + SSM_STATE, gi * gl:(gi + 1) * gl] = minc_ref[g, part].astype(BF16)
                mout_s[pr, r0:r0 + SSM_STATE, gi * gl:(gi + 1) * gl] = moutc_ref[g, part].astype(BF16)
        lane = lax.broadcasted_iota(jnp.int32, (SSM_GROUP, gl), 1)
        for g in range(PIECES):
            k = krow_ref[g]
            for j in range(SSM_T):
                blk = jnp.where(lane >= SSM_GROUP * j, pltpu.roll(k, SSM_GROUP * j, 1), 0.0) if j else k
                toep_s[g // SSM_PAIR, g % SSM_PAIR, j * SSM_GROUP:(j + 1) * SSM_GROUP, :] = blk.astype(BF16)

    for pr in range(npair):
        z_s[pr] = lax.dot_general(u_ref[pr], mint_s[pr], (((1,), (1,)), ((), ())),
                                  preferred_element_type=F32)

    lts = [(jnp.broadcast_to(lt_ref[pr, 0:1, :], (nb, sl)), jnp.broadcast_to(lt_ref[pr, 1:2, :], (nb, sl)))
           for pr in range(npair)]

    def step(c, carry):
        r0 = pl.multiple_of(c * nb, nb)
        out = []
        for pr in range(npair):
            s_re, s_im = carry[2 * pr], carry[2 * pr + 1]
            lt_re, lt_im = lts[pr]
            sp_s[pr, pl.ds(r0, nb), 0:sl] = s_re.astype(BF16)
            sp_s[pr, pl.ds(r0, nb), sl:2 * sl] = s_im.astype(BF16)
            z = z_s[pr, pl.ds(r0, nb), :]
            out.append(lt_re * s_re - lt_im * s_im + z[:, 0:sl])
            out.append(lt_re * s_im + lt_im * s_re + z[:, sl:2 * sl])
        return tuple(out)

    init = tuple(st_s[pr, part] for pr in range(npair) for part in range(2))
    fin = lax.fori_loop(0, rows // nb, step, init)
    for pr in range(npair):
        st_s[pr, 0] = fin[2 * pr]
        st_s[pr, 1] = fin[2 * pr + 1]

    for pr in range(npair):
        y = jnp.dot(sp_s[pr], mout_s[pr], preferred_element_type=F32)
        for gi in range(SSM_PAIR):
            cols = slice(gi * gl, (gi + 1) * gl)
            yg = y[:, cols] + jnp.dot(u_ref[pr, :, cols], toep_s[pr, gi], preferred_element_type=F32)
            y_ref[pr, :, cols] = _gelu_tanh(yg).astype(BF16)


def _ssm_operators(a_re, a_im, log_dt, b_re, b_im, c_re, c_im, d_skip):
    hp = lax.Precision.HIGHEST
    g, p = a_re.shape
    hg = b_re.shape[-1]
    t_len = SSM_T
    g2 = g // SSM_PAIR
    dt = jnp.exp(log_dt.astype(F32))[:, None]
    are, aim = a_re.astype(F32) * dt, a_im.astype(F32) * dt
    tt = jnp.arange(t_len + 1, dtype=F32)
    mag = jnp.exp(are[:, :, None] * tt)
    pw_re, pw_im = mag * jnp.cos(aim[:, :, None] * tt), mag * jnp.sin(aim[:, :, None] * tt)
    lb_re, lb_im = pw_re[:, :, 1], pw_im[:, :, 1]
    lr, li = a_re.astype(F32), a_im.astype(F32)
    den = lr * lr + li * li
    f_re = ((lb_re - 1.0) * lr + lb_im * li) / den
    f_im = (lb_im * lr - (lb_re - 1.0) * li) / den
    bb_re = f_re[..., None] * b_re - f_im[..., None] * b_im
    bb_im = f_re[..., None] * b_im + f_im[..., None] * b_re
    crt = jnp.transpose(c_re.astype(F32), (0, 2, 1))
    cit = jnp.transpose(c_im.astype(F32), (0, 2, 1))
    idx = jnp.arange(t_len * hg)
    rep = (idx[None, :] // hg == jnp.arange(t_len)[:, None]).astype(F32)
    til = (idx[None, :] % hg == jnp.arange(hg)[:, None]).astype(F32)
    over_steps = lambda v: jnp.einsum('gpt,tx->gpx', v, rep, precision=hp)
    over_chans = lambda v: jnp.einsum('gph,hx->gpx', v, til, precision=hp)
    c_x = over_chans(crt), over_chans(cit)
    b_x = over_chans(bb_re), over_chans(bb_im)

    def c_times_powers(lo):
        w_re, w_im = over_steps(pw_re[:, :, lo:lo + t_len]), over_steps(pw_im[:, :, lo:lo + t_len])
        return c_x[0] * w_re - c_x[1] * w_im, c_x[0] * w_im + c_x[1] * w_re

    cp_re, cp_im = c_times_powers(0)
    krow = jnp.einsum('gqx,gqb->gbx', jnp.concatenate([cp_re, -cp_im], axis=1),
                      jnp.concatenate([bb_re, bb_im], axis=1), precision=hp)
    lag0_diag = ((idx[None, :] // hg == 0) & (idx[None, :] % hg == jnp.arange(hg)[:, None])).astype(F32)
    krow = krow + lag0_diag[None] * jnp.einsum('ga,ax->gx', d_skip.astype(F32), til, precision=hp)[:, None, :]

    rv_re, rv_im = over_steps(pw_re[:, :, :t_len][:, :, ::-1]), over_steps(pw_im[:, :, :t_len][:, :, ::-1])
    m_in_t = jnp.stack([rv_re * b_x[0] - rv_im * b_x[1], rv_re * b_x[1] + rv_im * b_x[0]], axis=1)
    cn_re, cn_im = c_times_powers(1)
    m_out = jnp.stack([cn_re, -cn_im], axis=1)
    lt = jnp.stack([pw_re[:, :, t_len], pw_im[:, :, t_len]], axis=0)
    lt = jnp.transpose(lt.reshape(2, g2, SSM_PAIR * p), (1, 0, 2))
    return m_in_t, krow, m_out, lt


def _ssm(uf, b, s, ops):
    m_in_t, krow, m_out, lt = ops
    nc = s // SSM_T
    npair = PIECES // SSM_PAIR
    gl = SSM_T * SSM_GROUP
    wl = SSM_PAIR * gl
    sl2 = 2 * SSM_PAIR * SSM_STATE
    cb = _largest_divisor(nc, SSM_ROWS // b)
    rows = cb * b
    per = lambda n, *shape: pl.BlockSpec((n,) + shape, lambda bd, cr: (bd,) + (0,) * len(shape))
    io = pl.BlockSpec((npair, rows, wl), lambda bd, cr: (bd, cr, 0))
    return pl.pallas_call(
        functools.partial(_ssm_kernel, nb=b),
        grid=(uf.shape[0] // npair, nc // cb),
        in_specs=[io, per(PIECES, 2, SSM_STATE, gl), per(PIECES, SSM_GROUP, gl), per(PIECES, 2, SSM_STATE, gl),
                  per(npair, 2, sl2 // 2)],
        out_specs=io,
        out_shape=jax.ShapeDtypeStruct(uf.shape, BF16),
        scratch_shapes=[pltpu.VMEM((npair, 2, b, sl2 // 2), F32), pltpu.VMEM((npair, SSM_PAIR, gl, gl), BF16),
                        pltpu.VMEM((npair, sl2, wl), BF16), pltpu.VMEM((npair, sl2, wl), BF16),
                        pltpu.VMEM((npair, rows, sl2), F32), pltpu.VMEM((npair, rows, sl2), BF16)],
        compiler_params=pltpu.CompilerParams(
            dimension_semantics=("arbitrary", "arbitrary"), vmem_limit_bytes=VMEM_LIMIT),
        name="ssm_chunk",
    )(uf, m_in_t, krow, m_out, lt)


def _out_ffn_kernel(x_ref, att_ref, ys_ref, gate_ref, wa_ref, wg_ref, wb_ref, wo_ref,
                    gf_ref, w1_ref, w2_ref, gl_ref, o_ref, ys_s):
    nb, ts, d = x_ref.shape
    rows = nb * ts
    dot = lambda a, w: jnp.dot(a, w, preferred_element_type=F32)
    nbd = ys_s.shape[0]
    pitch = ys_s.shape[1] // nb
    gl = SSM_T * SSM_GROUP
    for bd in range(nbd):
        for jo in range(SSM_T // PIECES):
            tiles = []
            for g in range(PIECES):
                lo = (g % SSM_PAIR) * gl + jo * LANES
                tiles.append(ys_ref[bd * (PIECES // SSM_PAIR) + g // SSM_PAIR, :, lo:lo + LANES].astype(F32))
            for j8, tile in enumerate(_piece_transpose(tiles)):
                for c in range(ts // SSM_T):
                    t = c * SSM_T + jo * PIECES + j8
                    ys_s[bd, pl.ds(t, nb, stride=pitch), :] = tile[c * nb:(c + 1) * nb, :]
    ys = jnp.concatenate(
        [jnp.concatenate([ys_s[bd, bi * pitch:bi * pitch + ts, :] for bi in range(nb)], axis=0)
         for bd in range(nbd)], axis=1).astype(BF16)
    y_a = dot(att_ref[...].reshape(rows, att_ref.shape[2]), wa_ref[...])
    glu = dot(ys, wg_ref[...])
    sw = glu.shape[1] // 2
    y_b = dot((glu[:, :sw] * jax.nn.sigmoid(glu[:, sw:])).astype(BF16), wb_ref[...])
    gates = gate_ref[...].reshape(rows, 2 * d)
    mixed = gates[:, :d].astype(F32) * y_a + gates[:, d:].astype(F32) * y_b
    h = x_ref[...].reshape(rows, d) + dot(mixed.astype(BF16), wo_ref[...])
    n = _rms(h, gf_ref[...]).astype(BF16)
    dff = w1_ref.shape[1]
    fc = min(dff, 1024)
    acc = h
    for lo in range(0, dff, fc):
        f = jnp.maximum(dot(n, w1_ref[:, lo:lo + fc]), 0.0)
        acc = acc + dot((f * f).astype(BF16), w2_ref[lo:lo + fc, :])
    o_ref[...] = _rms(acc, gl_ref[...]).reshape(nb, ts, d)


def _out_ffn(x, att, ys, gates, w_a, w_glu, w_b, w_out, norm_ffn, w1, w2, norm_final, ts):
    b, s, d = x.shape
    npairs, _, pair_lanes = ys.shape
    nbd = npairs * SSM_PAIR // PIECES
    row = lambda w: pl.BlockSpec((b, ts, w), lambda i: (0, i, 0))
    weights = [w_a, w_glu, w_b, w_out]
    bf = [w.astype(BF16) for w in weights]
    w1b, w2b = w1.astype(BF16), w2.astype(BF16)
    return pl.pallas_call(
        _out_ffn_kernel,
        grid=(s // ts,),
        in_specs=[row(d), row(att.shape[2]),
                  pl.BlockSpec((npairs, (ts // SSM_T) * b, pair_lanes), lambda i: (0, i, 0)),
                  row(gates.shape[2])]
        + [_const_spec(w.shape) for w in bf]
        + [_const_spec((1, d)), _const_spec(w1b.shape), _const_spec(w2b.shape), _const_spec((1, d))],
        out_specs=row(d),
        out_shape=jax.ShapeDtypeStruct((b, s, d), F32),
        scratch_shapes=[pltpu.VMEM((nbd, b * _row_pitch(ts), LANES), F32)],
        compiler_params=pltpu.CompilerParams(
            dimension_semantics=("arbitrary",), vmem_limit_bytes=VMEM_LIMIT),
        name="out_ffn",
    )(x, att, ys, gates, *bf, norm_ffn.reshape(1, d), w1b, w2b, norm_final.reshape(1, d))


def _time_block(b, s, rows):
    ts = _largest_divisor(s // SSM_T, max(1, rows // (b * SSM_T))) * SSM_T
    assert ts % 16 == 0, "bf16 blocks need whole 16-row sublane tiles"
    return ts


def kernel(x, norm_mix, w_in, b_gate, rel_bias, ssm_a_re, ssm_a_im, ssm_log_dt, ssm_b_re, ssm_b_im,
           ssm_c_re, ssm_c_im, ssm_d, w_glu, w_proj_a, w_proj_b, w_out, norm_ffn, w_ff1, w_ff2,
           norm_final):
    b, s, d = x.shape
    assert norm_mix.shape[0] == 1, "single-layer block"
    assert s % Q_BLOCK == 0 and s >= KEY_WINDOW and s % SSM_T == 0
    assert b % 16 == 0, "batch rows fill whole bf16 sublane tiles in the SSM state scan"
    q, k, v, us, gates = _in_proj(x, norm_mix[0], w_in[0], b_gate[0], _time_block(b, s, IN_PROJ_ROWS))
    att = _band_attn(q, k, v, rel_bias[0])
    ops = _ssm_operators(ssm_a_re[0], ssm_a_im[0], ssm_log_dt[0], ssm_b_re[0], ssm_b_im[0],
                         ssm_c_re[0], ssm_c_im[0], ssm_d[0])
    ys = _ssm(us, b, s, ops)
    return _out_ffn(x, att, ys, gates, w_proj_a[0], w_glu[0], w_proj_b[0], w_out[0],
                    norm_ffn[0], w_ff1[0], w_ff2[0], norm_final, _time_block(b, s, OUT_FFN_ROWS))
```

```python
import functools
import math

import jax
import jax.numpy as jnp
from jax import lax
from jax.experimental import pallas as pl
from jax.experimental.pallas import tpu as pltpu

F32 = jnp.float32
BF16 = jnp.bfloat16

CHUNK = 64
N_LEFT_CHUNKS = 8
ATT_HEADS = 8
HEAD_DIM = 64
ATT_WIDTH = ATT_HEADS * HEAD_DIM
REL_CLIP = 128
SSM_GROUP = 16
SSM_STATE = 64
EPS = 1e-6
NEG_INF = -1e30

LANES = 128
SUBLANES = 8
BF16_TILE_ROWS = 16
Q_BLOCK = 4 * CHUNK
ATTN_SUBS = 16
KEY_WINDOW = Q_BLOCK + N_LEFT_CHUNKS * CHUNK
LEFT_BLOCKS = (N_LEFT_CHUNKS * CHUNK) // Q_BLOCK
BLOCK_DISTANCES = (N_LEFT_CHUNKS * CHUNK) // LANES + 1
SSM_T = 16
SSM_PAIR = 2
PIECES = LANES // SSM_GROUP
SSM_ROWS = 2048
IN_PROJ_ROWS = 1024
OUT_FFN_ROWS = 512
LOG2E = math.log2(math.e)
Q_SCALE = HEAD_DIM ** -0.5 * LOG2E
VMEM_LIMIT = 56 * 1024 * 1024


def _largest_divisor(n, cap):
    return max(d for d in range(1, min(n, cap) + 1) if n % d == 0)


def _row_pitch(ts):
    assert ts % SUBLANES == 0
    return ts + SUBLANES // 2


def _const_spec(shape):
    nd = len(shape)
    return pl.BlockSpec(shape, lambda *_: (0,) * nd, pipeline_mode=pl.Buffered(1))


def _rms(x, g):
    ms = jnp.mean(x * x, axis=-1, keepdims=True)
    return x * lax.rsqrt(ms + EPS) * g


def _in_proj_kernel(x_ref, g_ref, w_ref, bg_ref, q_ref, k_ref, v_ref, us_ref, gate_ref, us_s):
    nb, ts, d = x_ref.shape
    u = _rms(x_ref[...].reshape(nb * ts, d), g_ref[...]).astype(BF16)

    def proj(lo, hi):
        return jnp.dot(u, w_ref[:, lo:hi], preferred_element_type=F32)

    aw = ATT_WIDTH
    nbd = us_s.shape[0]
    pitch = us_s.shape[1] // nb
    gl = SSM_T * SSM_GROUP
    g0 = 3 * aw + nbd * LANES
    half = gate_ref.shape[2] // 2

    def gate(lo):
        z = proj(g0 + lo, g0 + lo + half) + bg_ref[:, lo:lo + half]
        gate_ref[:, :, lo:lo + half] = jax.nn.sigmoid(z).astype(BF16).reshape(nb, ts, half)

    gate(0)
    us = proj(3 * aw, 3 * aw + nbd * LANES)
    for bd in range(nbd):
        for bi in range(nb):
            us_s[bd, bi * pitch:bi * pitch + ts, :] = us[bi * ts:(bi + 1) * ts, bd * LANES:(bd + 1) * LANES]
    gate(half)
    for bd in range(nbd):
        for jo in range(SSM_T // PIECES):
            tiles = [jnp.concatenate([us_s[bd, pl.ds(c * SSM_T + jo * PIECES + j8, nb, stride=pitch), :]
                                      for c in range(ts // SSM_T)], axis=0) for j8 in range(PIECES)]
            for g, tile in enumerate(_piece_transpose(tiles)):
                lo = (g % SSM_PAIR) * gl + jo * LANES
                us_ref[bd * (PIECES // SSM_PAIR) + g // SSM_PAIR, :, lo:lo + LANES] = tile.astype(BF16)
    q_ref[...] = (proj(0, aw) * Q_SCALE).astype(BF16).reshape(nb, ts, aw)
    k_ref[...] = proj(aw, 2 * aw).astype(BF16).reshape(nb, ts, aw)
    v_ref[...] = proj(2 * aw, 3 * aw).astype(BF16).reshape(nb, ts, aw)


def _in_proj(x, norm_g, w_in, b_gate, ts):
    b, s, d = x.shape
    n_in = w_in.shape[1]
    gw = b_gate.shape[0]
    sw = n_in - 3 * ATT_WIDTH - gw
    nbd = sw // LANES
    row = lambda w: pl.BlockSpec((b, ts, w), lambda i: (0, i, 0))
    ssm_rows = (ts // SSM_T) * b
    npairs = sw // (SSM_PAIR * SSM_GROUP)
    pair_lanes = SSM_PAIR * SSM_T * SSM_GROUP
    return pl.pallas_call(
        _in_proj_kernel,
        grid=(s // ts,),
        in_specs=[row(d), _const_spec((1, d)), _const_spec((d, n_in)), _const_spec((1, gw))],
        out_specs=[row(ATT_WIDTH), row(ATT_WIDTH), row(ATT_WIDTH),
                   pl.BlockSpec((npairs, ssm_rows, pair_lanes), lambda i: (0, i, 0)), row(gw)],
        out_shape=[jax.ShapeDtypeStruct((b, s, ATT_WIDTH), BF16)] * 3
        + [jax.ShapeDtypeStruct((npairs, (s // SSM_T) * b, pair_lanes), BF16),
           jax.ShapeDtypeStruct((b, s, gw), BF16)],
        scratch_shapes=[pltpu.VMEM((nbd, b * _row_pitch(ts), LANES), F32)],
        compiler_params=pltpu.CompilerParams(
            dimension_semantics=("arbitrary",), vmem_limit_bytes=VMEM_LIMIT),
        name="in_proj",
    )(x, norm_g.reshape(1, d), w_in.astype(BF16), b_gate.reshape(1, gw))


def _build_bias_tiles(wrev_ref, tiles_s):
    qi = lax.broadcasted_iota(jnp.int32, (LANES, LANES), 0)
    ri = lax.broadcasted_iota(jnp.int32, (LANES, LANES), 1)
    below = ri <= qi
    dchunk = qi // CHUNK - ri // CHUNK
    per_block = LANES // CHUNK
    for h in range(ATT_HEADS):
        for d in range(BLOCK_DISTANCES):
            row = wrev_ref[pl.ds(h * BLOCK_DISTANCES + d, 1), :]
            lo = pltpu.roll(jnp.broadcast_to(row[:, :LANES], (LANES, LANES)), 1, 1, stride=1, stride_axis=0)
            hi = pltpu.roll(jnp.broadcast_to(row[:, LANES:], (LANES, LANES)), 1, 1, stride=1, stride_axis=0)
            dc = per_block * d + dchunk
            valid = (dc >= 0) & (dc <= N_LEFT_CHUNKS)
            tiles_s[d, h] = jnp.where(valid, jnp.where(below, lo, hi), NEG_INF)
        tiles_s[BLOCK_DISTANCES, h] = jnp.full((LANES, LANES), NEG_INF, F32)


def _attn_kernel(q_ref, k_ref, v_ref, wrev_ref, o_ref, tiles_s):
    @pl.when((pl.program_id(0) == 0) & (pl.program_id(1) == 0))
    def _():
        _build_bias_tiles(wrev_ref, tiles_s)

    subs = q_ref.shape[1] // Q_BLOCK
    for sub in range(subs):
        rows = slice(sub * Q_BLOCK, (sub + 1) * Q_BLOCK)
        _attn_block(q_ref.at[:, rows, :], k_ref, v_ref, o_ref.at[:, rows, :], tiles_s,
                    pl.program_id(1) * subs + sub)


def _attn_block(q_ref, k_ref, v_ref, o_ref, tiles_s, i):
    start = pl.multiple_of(jnp.maximum(i - LEFT_BLOCKS, 0) * Q_BLOCK, Q_BLOCK)
    base = jnp.minimum(i, LEFT_BLOCKS) * (Q_BLOCK // LANES)
    first_half = lax.broadcasted_iota(jnp.int32, (Q_BLOCK, LANES), 1) < HEAD_DIM

    def tile_index(qs, kb):
        d = base + qs - kb
        return jnp.where((d >= 0) & (d < BLOCK_DISTANCES), d, BLOCK_DISTANCES)

    for p in range(ATT_HEADS // 2):
        cols = slice(p * LANES, (p + 1) * LANES)
        qp = q_ref[0, :, cols]
        zero = jnp.zeros_like(qp)
        qm = jnp.concatenate([jnp.where(first_half, qp, zero), jnp.where(first_half, zero, qp)], axis=0)
        kp = k_ref[0, pl.ds(start, KEY_WINDOW), cols]
        vp = v_ref[0, pl.ds(start, KEY_WINDOW), cols]
        s = lax.dot_general(qm, kp, (((1,), (1,)), ((), ())), preferred_element_type=F32)
        bias = jnp.concatenate(
            [jnp.concatenate([tiles_s[tile_index(qs, kb), 2 * p + e] for kb in range(KEY_WINDOW // LANES)], axis=1)
             for e in range(2) for qs in range(Q_BLOCK // LANES)], axis=0)
        s = s + bias
        mx = jnp.max(s, axis=-1, keepdims=True)
        pe = jnp.exp2((s - mx).astype(BF16))
        oa = jnp.dot(pe, jnp.concatenate([vp, jnp.ones_like(vp)], axis=1), preferred_element_type=F32)
        o = oa[:, :LANES] / oa[:, LANES:]
        o_ref[0, :, cols] = jnp.where(first_half, o[:Q_BLOCK], o[Q_BLOCK:]).astype(BF16)


def _reversed_bias_windows(rel_bias):
    d = jnp.arange(BLOCK_DISTANCES)[:, None]
    m = jnp.arange(2 * LANES)[None, :]
    idx = jnp.clip(LANES * d + LANES - 1 - m, -REL_CLIP, REL_CLIP) + REL_CLIP
    return rel_bias.astype(F32)[:, idx].reshape(ATT_HEADS * BLOCK_DISTANCES, 2 * LANES)


def _band_attn(q, k, v, rel_bias):
    b, s, w = q.shape
    step_rows = _largest_divisor(s // Q_BLOCK, ATTN_SUBS) * Q_BLOCK
    wrev = _reversed_bias_windows(rel_bias) * LOG2E
    whole = pl.BlockSpec((1, s, w), lambda bi, i: (bi, 0, 0))
    return pl.pallas_call(
        _attn_kernel,
        grid=(b, s // step_rows),
        in_specs=[
            pl.BlockSpec((1, step_rows, w), lambda bi, i: (bi, i, 0)),
            whole, whole,
            pl.BlockSpec(wrev.shape, lambda bi, i: (0, 0)),
        ],
        out_specs=pl.BlockSpec((1, step_rows, w), lambda bi, i: (bi, i, 0)),
        out_shape=jax.ShapeDtypeStruct((b, s, w), BF16),
        scratch_shapes=[pltpu.VMEM((BLOCK_DISTANCES + 1, ATT_HEADS, LANES, LANES), F32)],
        compiler_params=pltpu.CompilerParams(
            dimension_semantics=("arbitrary", "arbitrary"), vmem_limit_bytes=VMEM_LIMIT),
        name="band_attn",
    )(q, k, v, wrev)


def _piece_transpose(tiles):
    lane = lax.broadcasted_iota(jnp.int32, tiles[0].shape, 1)
    t = list(tiles)
    for s in (4, 2, 1):
        keep = ((lane // SSM_GROUP) & s) == 0
        sh = s * SSM_GROUP
        nxt = list(t)
        for i in range(PIECES):
            if i & s == 0:
                a, b = t[i], t[i + s]
                nxt[i] = jnp.where(keep, a, pltpu.roll(b, sh, 1))
                nxt[i + s] = jnp.where(keep, pltpu.roll(a, LANES - sh, 1), b)
        t = nxt
    return t


def _gelu_tanh(x):
    a = -2.0 * math.sqrt(2.0 / math.pi) * LOG2E
    return x / (1.0 + jnp.exp2(x * (a + (a * 0.044715) * (x * x))))


def _ssm_kernel(u_ref, minc_ref, krow_ref, moutc_ref, lt_ref, y_ref, st_s, toep_s, mint_s, mout_s, z_s, sp_s,
                *, nb):
    npair = mint_s.shape[0]
    rows = u_ref.shape[1]
    sl = SSM_PAIR * SSM_STATE
    gl = SSM_T * SSM_GROUP

    @pl.when(pl.program_id(1) == 0)
    def _():
        st_s[...] = jnp.zeros_like(st_s)
        mint_s[...] = jnp.zeros_like(mint_s)
        mout_s[...] = jnp.zeros_like(mout_s)
        for g in range(PIECES):
            pr, gi = divmod(g, SSM_PAIR)
            for part in range(2):
                r0 = (part * SSM_PAIR + gi) * SSM_STATE
                mint_s[pr, r0:r0 + SSM_STATE, gi * gl:(gi + 1) * gl] = minc_ref[g, part].astype(BF16)
                mout_s[pr, r0:r0 + SSM_STATE, gi * gl:(gi + 1) * gl] = moutc_ref[g, part].astype(BF16)
        lane = lax.broadcasted_iota(jnp.int32, (SSM_GROUP, gl), 1)
        for g in range(PIECES):
            k = krow_ref[g]
            for j in range(SSM_T):
                blk = jnp.where(lane >= SSM_GROUP * j, pltpu.roll(k, SSM_GROUP * j, 1), 0.0) if j else k
                toep_s[g // SSM_PAIR, g % SSM_PAIR, j * SSM_GROUP:(j + 1) * SSM_GROUP, :] = blk.astype(BF16)

    for pr in range(npair):
        z_s[pr] = lax.dot_general(u_ref[pr], mint_s[pr], (((1,), (1,)), ((), ())),
                                  preferred_element_type=F32)

    lts = [(jnp.broadcast_to(lt_ref[pr, 0:1, :], (nb, sl)), jnp.broadcast_to(lt_ref[pr, 1:2, :], (nb, sl)))
           for pr in range(npair)]

    def step(c, carry):
        r0 = pl.multiple_of(c * nb, nb)
        out = []
        for pr in range(npair):
            s_re, s_im = carry[2 * pr], carry[2 * pr + 1]
            lt_re, lt_im = lts[pr]
            sp_s[pr, pl.ds(r0, nb), 0:sl] = s_re.astype(BF16)
            sp_s[pr, pl.ds(r0, nb), sl:2 * sl] = s_im.astype(BF16)
            z = z_s[pr, pl.ds(r0, nb), :]
            out.append(lt_re * s_re - lt_im * s_im + z[:, 0:sl])
            out.append(lt_re * s_im + lt_im * s_re + z[:, sl:2 * sl])
        return tuple(out)

    init = tuple(st_s[pr, part] for pr in range(npair) for part in range(2))
    fin = lax.fori_loop(0, rows // nb, step, init)
    for pr in range(npair):
        st_s[pr, 0] = fin[2 * pr]
        st_s[pr, 1] = fin[2 * pr + 1]

    for pr in range(npair):
        y = jnp.dot(sp_s[pr], mout_s[pr], preferred_element_type=F32)
        for gi in range(SSM_PAIR):
            cols = slice(gi * gl, (gi + 1) * gl)
            yg = y[:, cols] + jnp.dot(u_ref[pr, :, cols], toep_s[pr, gi], preferred_element_type=F32)
            y_ref[pr, :, cols] = _gelu_tanh(yg).astype(BF16)


def _ssm_operators(a_re, a_im, log_dt, b_re, b_im, c_re, c_im, d_skip):
    hp = lax.Precision.HIGHEST
    g, p = a_re.shape
    hg = b_re.shape[-1]
    t_len = SSM_T
    g2 = g // SSM_PAIR
    dt = jnp.exp(log_dt.astype(F32))[:, None]
    are, aim = a_re.astype(F32) * dt, a_im.astype(F32) * dt
    tt = jnp.arange(t_len + 1, dtype=F32)
    mag = jnp.exp(are[:, :, None] * tt)
    pw_re, pw_im = mag * jnp.cos(aim[:, :, None] * tt), mag * jnp.sin(aim[:, :, None] * tt)
    lb_re, lb_im = pw_re[:, :, 1], pw_im[:, :, 1]
    lr, li = a_re.astype(F32), a_im.astype(F32)
    den = lr * lr + li * li
    f_re = ((lb_re - 1.0) * lr + lb_im * li) / den
    f_im = (lb_im * lr - (lb_re - 1.0) * li) / den
    bb_re = f_re[..., None] * b_re - f_im[..., None] * b_im
    bb_im = f_re[..., None] * b_im + f_im[..., None] * b_re
    crt = jnp.transpose(c_re.astype(F32), (0, 2, 1))
    cit = jnp.transpose(c_im.astype(F32), (0, 2, 1))
    idx = jnp.arange(t_len * hg)
    rep = (idx[None, :] // hg == jnp.arange(t_len)[:, None]).astype(F32)
    til = (idx[None, :] % hg == jnp.arange(hg)[:, None]).astype(F32)
    over_steps = lambda v: jnp.einsum('gpt,tx->gpx', v, rep, precision=hp)
    over_chans = lambda v: jnp.einsum('gph,hx->gpx', v, til, precision=hp)
    c_x = over_chans(crt), over_chans(cit)
    b_x = over_chans(bb_re), over_chans(bb_im)

    def c_times_powers(lo):
        w_re, w_im = over_steps(pw_re[:, :, lo:lo + t_len]), over_steps(pw_im[:, :, lo:lo + t_len])
        return c_x[0] * w_re - c_x[1] * w_im, c_x[0] * w_im + c_x[1] * w_re

    cp_re, cp_im = c_times_powers(0)
    krow = jnp.einsum('gqx,gqb->gbx', jnp.concatenate([cp_re, -cp_im], axis=1),
                      jnp.concatenate([bb_re, bb_im], axis=1), precision=hp)
    lag0_diag = ((idx[None, :] // hg == 0) & (idx[None, :] % hg == jnp.arange(hg)[:, None])).astype(F32)
    krow = krow + lag0_diag[None] * jnp.einsum('ga,ax->gx', d_skip.astype(F32), til, precision=hp)[:, None, :]

    rv_re, rv_im = over_steps(pw_re[:, :, :t_len][:, :, ::-1]), over_steps(pw_im[:, :, :t_len][:, :, ::-1])
    m_in_t = jnp.stack([rv_re * b_x[0] - rv_im * b_x[1], rv_re * b_x[1] + rv_im * b_x[0]], axis=1)
    cn_re, cn_im = c_times_powers(1)
    m_out = jnp.stack([cn_re, -cn_im], axis=1)
    lt = jnp.stack([pw_re[:, :, t_len], pw_im[:, :, t_len]], axis=0)
    lt = jnp.transpose(lt.reshape(2, g2, SSM_PAIR * p), (1, 0, 2))
    return m_in_t, krow, m_out, lt


def _ssm(uf, b, s, ops):
    m_in_t, krow, m_out, lt = ops
    nc = s // SSM_T
    npair = PIECES // SSM_PAIR
    gl = SSM_T * SSM_GROUP
    wl = SSM_PAIR * gl
    sl2 = 2 * SSM_PAIR * SSM_STATE
    cb = _largest_divisor(nc, SSM_ROWS // b)
    rows = cb * b
    per = lambda n, *shape: pl.BlockSpec((n,) + shape, lambda bd, cr: (bd,) + (0,) * len(shape))
    io = pl.BlockSpec((npair, rows, wl), lambda bd, cr: (bd, cr, 0))
    return pl.pallas_call(
        functools.partial(_ssm_kernel, nb=b),
        grid=(uf.shape[0] // npair, nc // cb),
        in_specs=[io, per(PIECES, 2, SSM_STATE, gl), per(PIECES, SSM_GROUP, gl), per(PIECES, 2, SSM_STATE, gl),
                  per(npair, 2, sl2 // 2)],
        out_specs=io,
        out_shape=jax.ShapeDtypeStruct(uf.shape, BF16),
        scratch_shapes=[pltpu.VMEM((npair, 2, b, sl2 // 2), F32), pltpu.VMEM((npair, SSM_PAIR, gl, gl), BF16),
                        pltpu.VMEM((npair, sl2, wl), BF16), pltpu.VMEM((npair, sl2, wl), BF16),
                        pltpu.VMEM((npair, rows, sl2), F32), pltpu.VMEM((npair, rows, sl2), BF16)],
        compiler_params=pltpu.CompilerParams(
            dimension_semantics=("arbitrary", "arbitrary"), vmem_limit_bytes=VMEM_LIMIT),
        name="ssm_chunk",
    )(uf, m_in_t, krow, m_out, lt)


def _out_ffn_kernel(x_ref, att_ref, ys_ref, gate_ref, wa_ref, wg_ref, wb_ref, wo_ref,
                    gf_ref, w1_ref, w2_ref, gl_ref, o_ref, ys_s):
    nb, ts, d = x_ref.shape
    rows = nb * ts
    dot = lambda a, w: jnp.dot(a, w, preferred_element_type=F32)
    nbd = ys_s.shape[0]
    pitch = ys_s.shape[1] // nb
    gl = SSM_T * SSM_GROUP
    for bd in range(nbd):
        for jo in range(SSM_T // PIECES):
            tiles = []
            for g in range(PIECES):
                lo = (g % SSM_PAIR) * gl + jo * LANES
                tiles.append(ys_ref[bd * (PIECES // SSM_PAIR) + g // SSM_PAIR, :, lo:lo + LANES].astype(F32))
            for j8, tile in enumerate(_piece_transpose(tiles)):
                for c in range(ts // SSM_T):
                    t = c * SSM_T + jo * PIECES + j8
                    ys_s[bd, pl.ds(t, nb, stride=pitch), :] = tile[c * nb:(c + 1) * nb, :]
    ys = jnp.concatenate(
        [jnp.concatenate([ys_s[bd, bi * pitch:bi * pitch + ts, :] for bi in range(nb)], axis=0)
         for bd in range(nbd)], axis=1).astype(BF16)
    y_a = dot(att_ref[...].reshape(rows, att_ref.shape[2]), wa_ref[...])
    glu = dot(ys, wg_ref[...])
    sw = glu.shape[1] // 2
    y_b = dot((glu[:, :sw] * jax.nn.sigmoid(glu[:, sw:])).astype(BF16), wb_ref[...])
    gates = gate_ref[...].reshape(rows, 2 * d)
    mixed = gates[:, :d].astype(F32) * y_a + gates[:, d:].astype(F32) * y_b
    h = x_ref[...].reshape(rows, d) + dot(mixed.astype(BF16), wo_ref[...])
    n = _rms(h, gf_ref[...]).astype(BF16)
    dff = w1_ref.shape[1]
    fc = min(dff, 1024)
    acc = h
    for lo in range(0, dff, fc):
        f = jnp.maximum(dot(n, w1_ref[:, lo:lo + fc]), 0.0)
        acc = acc + dot((f * f).astype(BF16), w2_ref[lo:lo + fc, :])
    o_ref[...] = _rms(acc, gl_ref[...]).reshape(nb, ts, d)


def _out_ffn(x, att, ys, gates, w_a, w_glu, w_b, w_out, norm_ffn, w1, w2, norm_final, ts):
    b, s, d = x.shape
    npairs, _, pair_lanes = ys.shape
    nbd = npairs * SSM_PAIR // PIECES
    row = lambda w: pl.BlockSpec((b, ts, w), lambda i: (0, i, 0))
    weights = [w_a, w_glu, w_b, w_out]
    bf = [w.astype(BF16) for w in weights]
    w1b, w2b = w1.astype(BF16), w2.astype(BF16)
    return pl.pallas_call(
        _out_ffn_kernel,
        grid=(s // ts,),
        in_specs=[row(d), row(att.shape[2]),
                  pl.BlockSpec((npairs, (ts // SSM_T) * b, pair_lanes), lambda i: (0, i, 0)),
                  row(gates.shape[2])]
        + [_const_spec(w.shape) for w in bf]
        + [_const_spec((1, d)), _const_spec(w1b.shape), _const_spec(w2b.shape), _const_spec((1, d))],
        out_specs=row(d),
        out_shape=jax.ShapeDtypeStruct((b, s, d), F32),
        scratch_shapes=[pltpu.VMEM((nbd, b * _row_pitch(ts), LANES), F32)],
        compiler_params=pltpu.CompilerParams(
            dimension_semantics=("arbitrary",), vmem_limit_bytes=VMEM_LIMIT),
        name="out_ffn",
    )(x, att, ys, gates, *bf, norm_ffn.reshape(1, d), w1b, w2b, norm_final.reshape(1, d))


def _time_block(b, s, rows):
    ts = _largest_divisor(s // SSM_T, max(1, rows // (b * SSM_T))) * SSM_T
    assert ts % BF16_TILE_ROWS == 0, "bf16 blocks need whole packed sublane tiles"
    return ts


def kernel(x, norm_mix, w_in, b_gate, rel_bias, ssm_a_re, ssm_a_im, ssm_log_dt, ssm_b_re, ssm_b_im,
           ssm_c_re, ssm_c_im, ssm_d, w_glu, w_proj_a, w_proj_b, w_out, norm_ffn, w_ff1, w_ff2,
           norm_final):
    b, s, d = x.shape
    assert norm_mix.shape[0] == 1, "single-layer block"
    assert s % Q_BLOCK == 0 and s >= KEY_WINDOW and s % SSM_T == 0
    assert b % BF16_TILE_ROWS == 0, "batch rows fill whole bf16 sublane tiles in the SSM state scan"
    q, k, v, us, gates = _in_proj(x, norm_mix[0], w_in[0], b_gate[0], _time_block(b, s, IN_PROJ_ROWS))
    att = _band_attn(q, k, v, rel_bias[0])
    ops = _ssm_operators(ssm_a_re[0], ssm_a_im[0], ssm_log_dt[0], ssm_b_re[0], ssm_b_im[0],
                         ssm_c_re[0], ssm_c_im[0], ssm_d[0])
    ys = _ssm(us, b, s, ops)
    return _out_ffn(x, att, ys, gates, w_proj_a[0], w_glu[0], w_proj_b[0], w_out[0],
                    norm_ffn[0], w_ff1[0], w_ff2[0], norm_final, _time_block(b, s, OUT_FFN_ROWS))
```

```python
import functools
import math

import jax
import jax.numpy as jnp
from jax import lax
from jax.experimental import pallas as pl
from jax.experimental.pallas import tpu as pltpu

F32 = jnp.float32
BF16 = jnp.bfloat16

CHUNK = 64
N_LEFT_CHUNKS = 8
ATT_HEADS = 8
HEAD_DIM = 64
ATT_WIDTH = ATT_HEADS * HEAD_DIM
REL_CLIP = 128
SSM_GROUP = 16
SSM_STATE = 64
EPS = 1e-6
NEG_INF = -1e30

LANES = 128
SUBLANES = 8
BF16_TILE_ROWS = 16
Q_BLOCK = 4 * CHUNK
ATTN_SUBS = 8
KEY_WINDOW = Q_BLOCK + N_LEFT_CHUNKS * CHUNK
LEFT_BLOCKS = (N_LEFT_CHUNKS * CHUNK) // Q_BLOCK
BLOCK_DISTANCES = (N_LEFT_CHUNKS * CHUNK) // LANES + 1
SSM_T = 16
SSM_PAIR = 2
PIECES = LANES // SSM_GROUP
SSM_ROWS = 1024
IN_PROJ_ROWS = 1024
OUT_FFN_ROWS = 512
LOG2E = math.log2(math.e)
Q_SCALE = HEAD_DIM ** -0.5 * LOG2E
VMEM_LIMIT = 56 * 1024 * 1024


def _largest_divisor(n, cap):
    return max(d for d in range(1, min(n, cap) + 1) if n % d == 0)


def _row_pitch(ts):
    assert ts % SUBLANES == 0
    return ts + SUBLANES // 2


def _const_spec(shape):
    nd = len(shape)
    return pl.BlockSpec(shape, lambda *_: (0,) * nd, pipeline_mode=pl.Buffered(1))


def _rms(x, g):
    ms = jnp.mean(x * x, axis=-1, keepdims=True)
    return x * lax.rsqrt(ms + EPS) * g


def _in_proj_kernel(x_ref, g_ref, w_ref, bg_ref, q_ref, k_ref, v_ref, us_ref, gate_ref, us_s):
    nb, ts, d = x_ref.shape
    u = _rms(x_ref[...].reshape(nb * ts, d), g_ref[...]).astype(BF16)

    def proj(lo, hi):
        return jnp.dot(u, w_ref[:, lo:hi], preferred_element_type=F32)

    aw = ATT_WIDTH
    nbd = us_s.shape[0]
    pitch = us_s.shape[1] // nb
    gl = SSM_T * SSM_GROUP
    g0 = 3 * aw + nbd * LANES
    half = gate_ref.shape[2] // 2

    def gate(lo):
        z = proj(g0 + lo, g0 + lo + half) + bg_ref[:, lo:lo + half]
        gate_ref[:, :, lo:lo + half] = jax.nn.sigmoid(z).astype(BF16).reshape(nb, ts, half)

    gate(0)
    us = proj(3 * aw, 3 * aw + nbd * LANES)
    for bd in range(nbd):
        for bi in range(nb):
            us_s[bd, bi * pitch:bi * pitch + ts, :] = us[bi * ts:(bi + 1) * ts, bd * LANES:(bd + 1) * LANES]
    gate(half)
    for bd in range(nbd):
        for jo in range(SSM_T // PIECES):
            tiles = [jnp.concatenate([us_s[bd, pl.ds(c * SSM_T + jo * PIECES + j8, nb, stride=pitch), :]
                                      for c in range(ts // SSM_T)], axis=0) for j8 in range(PIECES)]
            for g, tile in enumerate(_piece_transpose(tiles)):
                lo = (g % SSM_PAIR) * gl + jo * LANES
                us_ref[bd * (PIECES // SSM_PAIR) + g // SSM_PAIR, :, lo:lo + LANES] = tile.astype(BF16)
    q_ref[...] = (proj(0, aw) * Q_SCALE).astype(BF16).reshape(nb, ts, aw)
    k_ref[...] = proj(aw, 2 * aw).astype(BF16).reshape(nb, ts, aw)
    v_ref[...] = proj(2 * aw, 3 * aw).astype(BF16).reshape(nb, ts, aw)


def _in_proj(x, norm_g, w_in, b_gate, ts):
    b, s, d = x.shape
    n_in = w_in.shape[1]
    gw = b_gate.shape[0]
    sw = n_in - 3 * ATT_WIDTH - gw
    nbd = sw // LANES
    row = lambda w: pl.BlockSpec((b, ts, w), lambda i: (0, i, 0))
    ssm_rows = (ts // SSM_T) * b
    npairs = sw // (SSM_PAIR * SSM_GROUP)
    pair_lanes = SSM_PAIR * SSM_T * SSM_GROUP
    return pl.pallas_call(
        _in_proj_kernel,
        grid=(s // ts,),
        in_specs=[row(d), _const_spec((1, d)), _const_spec((d, n_in)), _const_spec((1, gw))],
        out_specs=[row(ATT_WIDTH), row(ATT_WIDTH), row(ATT_WIDTH),
                   pl.BlockSpec((npairs, ssm_rows, pair_lanes), lambda i: (0, i, 0)), row(gw)],
        out_shape=[jax.ShapeDtypeStruct((b, s, ATT_WIDTH), BF16)] * 3
        + [jax.ShapeDtypeStruct((npairs, (s // SSM_T) * b, pair_lanes), BF16),
           jax.ShapeDtypeStruct((b, s, gw), BF16)],
        scratch_shapes=[pltpu.VMEM((nbd, b * _row_pitch(ts), LANES), F32)],
        compiler_params=pltpu.CompilerParams(
            dimension_semantics=("arbitrary",), vmem_limit_bytes=VMEM_LIMIT),
        name="in_proj",
    )(x, norm_g.reshape(1, d), w_in.astype(BF16), b_gate.reshape(1, gw))


def _build_bias_tiles(wrev_ref, tiles_s):
    qi = lax.broadcasted_iota(jnp.int32, (LANES, LANES), 0)
    ri = lax.broadcasted_iota(jnp.int32, (LANES, LANES), 1)
    below = ri <= qi
    dchunk = qi // CHUNK - ri // CHUNK
    per_block = LANES // CHUNK
    for h in range(ATT_HEADS):
        for d in range(BLOCK_DISTANCES):
            row = wrev_ref[pl.ds(h * BLOCK_DISTANCES + d, 1), :]
            lo = pltpu.roll(jnp.broadcast_to(row[:, :LANES], (LANES, LANES)), 1, 1, stride=1, stride_axis=0)
            hi = pltpu.roll(jnp.broadcast_to(row[:, LANES:], (LANES, LANES)), 1, 1, stride=1, stride_axis=0)
            dc = per_block * d + dchunk
            valid = (dc >= 0) & (dc <= N_LEFT_CHUNKS)
            tiles_s[d, h] = jnp.where(valid, jnp.where(below, lo, hi), NEG_INF)
        tiles_s[BLOCK_DISTANCES, h] = jnp.full((LANES, LANES), NEG_INF, F32)


def _attn_kernel(q_ref, k_ref, v_ref, wrev_ref, o_ref, tiles_s):
    @pl.when((pl.program_id(0) == 0) & (pl.program_id(1) == 0))
    def _():
        _build_bias_tiles(wrev_ref, tiles_s)

    subs = q_ref.shape[1] // Q_BLOCK
    for sub in range(subs):
        rows = slice(sub * Q_BLOCK, (sub + 1) * Q_BLOCK)
        _attn_block(q_ref.at[:, rows, :], k_ref, v_ref, o_ref.at[:, rows, :], tiles_s,
                    pl.program_id(1) * subs + sub)


def _attn_block(q_ref, k_ref, v_ref, o_ref, tiles_s, i):
    start = pl.multiple_of(jnp.maximum(i - LEFT_BLOCKS, 0) * Q_BLOCK, Q_BLOCK)
    base = jnp.minimum(i, LEFT_BLOCKS) * (Q_BLOCK // LANES)
    first_half = lax.broadcasted_iota(jnp.int32, (Q_BLOCK, LANES), 1) < HEAD_DIM

    def tile_index(qs, kb):
        d = base + qs - kb
        return jnp.where((d >= 0) & (d < BLOCK_DISTANCES), d, BLOCK_DISTANCES)

    for p in range(ATT_HEADS // 2):
        cols = slice(p * LANES, (p + 1) * LANES)
        qp = q_ref[0, :, cols]
        zero = jnp.zeros_like(qp)
        qm = jnp.concatenate([jnp.where(first_half, qp, zero), jnp.where(first_half, zero, qp)], axis=0)
        kp = k_ref[0, pl.ds(start, KEY_WINDOW), cols]
        vp = v_ref[0, pl.ds(start, KEY_WINDOW), cols]
        s = lax.dot_general(qm, kp, (((1,), (1,)), ((), ())), preferred_element_type=F32)
        bias = jnp.concatenate(
            [jnp.concatenate([tiles_s[tile_index(qs, kb), 2 * p + e] for kb in range(KEY_WINDOW // LANES)], axis=1)
             for e in range(2) for qs in range(Q_BLOCK // LANES)], axis=0)
        s = s + bias
        mx = jnp.max(s, axis=-1, keepdims=True)
        pe = jnp.exp2((s - mx).astype(BF16))
        oa = jnp.dot(pe, jnp.concatenate([vp, jnp.ones_like(vp)], axis=1), preferred_element_type=F32)
        o = oa[:, :LANES] / oa[:, LANES:]
        o_ref[0, :, cols] = jnp.where(first_half, o[:Q_BLOCK], o[Q_BLOCK:]).astype(BF16)


def _reversed_bias_windows(rel_bias):
    d = jnp.arange(BLOCK_DISTANCES)[:, None]
    m = jnp.arange(2 * LANES)[None, :]
    idx = jnp.clip(LANES * d + LANES - 1 - m, -REL_CLIP, REL_CLIP) + REL_CLIP
    return rel_bias.astype(F32)[:, idx].reshape(ATT_HEADS * BLOCK_DISTANCES, 2 * LANES)


def _band_attn(q, k, v, rel_bias):
    b, s, w = q.shape
    step_rows = _largest_divisor(s // Q_BLOCK, ATTN_SUBS) * Q_BLOCK
    wrev = _reversed_bias_windows(rel_bias) * LOG2E
    whole = pl.BlockSpec((1, s, w), lambda bi, i: (bi, 0, 0))
    return pl.pallas_call(
        _attn_kernel,
        grid=(b, s // step_rows),
        in_specs=[
            pl.BlockSpec((1, step_rows, w), lambda bi, i: (bi, i, 0)),
            whole, whole,
            pl.BlockSpec(wrev.shape, lambda bi, i: (0, 0)),
        ],
        out_specs=pl.BlockSpec((1, step_rows, w), lambda bi, i: (bi, i, 0)),
        out_shape=jax.ShapeDtypeStruct((b, s, w), BF16),
        scratch_shapes=[pltpu.VMEM((BLOCK_DISTANCES + 1, ATT_HEADS, LANES, LANES), F32)],
        compiler_params=pltpu.CompilerParams(
            dimension_semantics=("arbitrary", "arbitrary"), vmem_limit_bytes=VMEM_LIMIT),
        name="band_attn",
    )(q, k, v, wrev)


def _piece_transpose(tiles):
    lane = lax.broadcasted_iota(jnp.int32, tiles[0].shape, 1)
    t = list(tiles)
    for s in (4, 2, 1):
        keep = ((lane // SSM_GROUP) & s) == 0
        sh = s * SSM_GROUP
        nxt = list(t)
        for i in range(PIECES):
            if i & s == 0:
                a, b = t[i], t[i + s]
                nxt[i] = jnp.where(keep, a, pltpu.roll(b, sh, 1))
                nxt[i + s] = jnp.where(keep, pltpu.roll(a, LANES - sh, 1), b)
        t = nxt
    return t


def _gelu_tanh(x):
    a = -2.0 * math.sqrt(2.0 / math.pi) * LOG2E
    return x / (1.0 + jnp.exp2(x * (a + (a * 0.044715) * (x * x))))


def _ssm_kernel(u_ref, minc_ref, krow_ref, moutc_ref, lt_ref, y_ref, st_s, toep_s, mint_s, mout_s, z_s, sp_s,
                *, nb):
    npair = mint_s.shape[0]
    rows = u_ref.shape[1]
    sl = SSM_PAIR * SSM_STATE
    gl = SSM_T * SSM_GROUP

    @pl.when(pl.program_id(1) == 0)
    def _():
        st_s[...] = jnp.zeros_like(st_s)
        mint_s[...] = jnp.zeros_like(mint_s)
        mout_s[...] = jnp.zeros_like(mout_s)
        for g in range(PIECES):
            pr, gi = divmod(g, SSM_PAIR)
            for part in range(2):
                r0 = (part * SSM_PAIR + gi) * SSM_STATE
                mint_s[pr, r0:r0 + SSM_STATE, gi * gl:(gi + 1) * gl] = minc_ref[g, part].astype(BF16)
                mout_s[pr, r0:r0 + SSM_STATE, gi * gl:(gi + 1) * gl] = moutc_ref[g, part].astype(BF16)
        lane = lax.broadcasted_iota(jnp.int32, (SSM_GROUP, gl), 1)
        for g in range(PIECES):
            k = krow_ref[g]
            for j in range(SSM_T):
                blk = jnp.where(lane >= SSM_GROUP * j, pltpu.roll(k, SSM_GROUP * j, 1), 0.0) if j else k
                toep_s[g // SSM_PAIR, g % SSM_PAIR, j * SSM_GROUP:(j + 1) * SSM_GROUP, :] = blk.astype(BF16)

    for pr in range(npair):
        z_s[pr] = lax.dot_general(u_ref[pr], mint_s[pr], (((1,), (1,)), ((), ())),
                                  preferred_element_type=F32)

    lts = [(jnp.broadcast_to(lt_ref[pr, 0:1, :], (nb, sl)), jnp.broadcast_to(lt_ref[pr, 1:2, :], (nb, sl)))
           for pr in range(npair)]

    def step(c, carry):
        r0 = pl.multiple_of(c * nb, nb)
        out = []
        for pr in range(npair):
            s_re, s_im = carry[2 * pr], carry[2 * pr + 1]
            lt_re, lt_im = lts[pr]
            sp_s[pr, pl.ds(r0, nb), 0:sl] = s_re.astype(BF16)
            sp_s[pr, pl.ds(r0, nb), sl:2 * sl] = s_im.astype(BF16)
            z = z_s[pr, pl.ds(r0, nb), :]
            out.append(lt_re * s_re - lt_im * s_im + z[:, 0:sl])
            out.append(lt_re * s_im + lt_im * s_re + z[:, sl:2 * sl])
        return tuple(out)

    init = tuple(st_s[pr, part] for pr in range(npair) for part in range(2))
    fin = lax.fori_loop(0, rows // nb, step, init)
    for pr in range(npair):
        st_s[pr, 0] = fin[2 * pr]
        st_s[pr, 1] = fin[2 * pr + 1]

    for pr in range(npair):
        y = jnp.dot(sp_s[pr], mout_s[pr], preferred_element_type=F32)
        for gi in range(SSM_PAIR):
            cols = slice(gi * gl, (gi + 1) * gl)
            yg = y[:, cols] + jnp.dot(u_ref[pr, :, cols], toep_s[pr, gi], preferred_element_type=F32)
            y_ref[pr, :, cols] = _gelu_tanh(yg).astype(BF16)


def _ssm_operators(a_re, a_im, log_dt, b_re, b_im, c_re, c_im, d_skip):
    hp = lax.Precision.HIGHEST
    g, p = a_re.shape
    hg = b_re.shape[-1]
    t_len = SSM_T
    g2 = g // SSM_PAIR
    dt = jnp.exp(log_dt.astype(F32))[:, None]
    are, aim = a_re.astype(F32) * dt, a_im.astype(F32) * dt
    tt = jnp.arange(t_len + 1, dtype=F32)
    mag = jnp.exp(are[:, :, None] * tt)
    pw_re, pw_im = mag * jnp.cos(aim[:, :, None] * tt), mag * jnp.sin(aim[:, :, None] * tt)
    lb_re, lb_im = pw_re[:, :, 1], pw_im[:, :, 1]
    lr, li = a_re.astype(F32), a_im.astype(F32)
    den = lr * lr + li * li
    f_re = ((lb_re - 1.0) * lr + lb_im * li) / den
    f_im = (lb_im * lr - (lb_re - 1.0) * li) / den
    bb_re = f_re[..., None] * b_re - f_im[..., None] * b_im
    bb_im = f_re[..., None] * b_im + f_im[..., None] * b_re
    crt = jnp.transpose(c_re.astype(F32), (0, 2, 1))
    cit = jnp.transpose(c_im.astype(F32), (0, 2, 1))
    idx = jnp.arange(t_len * hg)
    rep = (idx[None, :] // hg == jnp.arange(t_len)[:, None]).astype(F32)
    til = (idx[None, :] % hg == jnp.arange(hg)[:, None]).astype(F32)
    over_steps = lambda v: jnp.einsum('gpt,tx->gpx', v, rep, precision=hp)
    over_chans = lambda v: jnp.einsum('gph,hx->gpx', v, til, precision=hp)
    c_x = over_chans(crt), over_chans(cit)
    b_x = over_chans(bb_re), over_chans(bb_im)

    def c_times_powers(lo):
        w_re, w_im = over_steps(pw_re[:, :, lo:lo + t_len]), over_steps(pw_im[:, :, lo:lo + t_len])
        return c_x[0] * w_re - c_x[1] * w_im, c_x[0] * w_im + c_x[1] * w_re

    cp_re, cp_im = c_times_powers(0)
    krow = jnp.einsum('gqx,gqb->gbx', jnp.concatenate([cp_re, -cp_im], axis=1),
                      jnp.concatenate([bb_re, bb_im], axis=1), precision=hp)
    lag0_diag = ((idx[None, :] // hg == 0) & (idx[None, :] % hg == jnp.arange(hg)[:, None])).astype(F32)
    krow = krow + lag0_diag[None] * jnp.einsum('ga,ax->gx', d_skip.astype(F32), til, precision=hp)[:, None, :]

    rv_re, rv_im = over_steps(pw_re[:, :, :t_len][:, :, ::-1]), over_steps(pw_im[:, :, :t_len][:, :, ::-1])
    m_in_t = jnp.stack([rv_re * b_x[0] - rv_im * b_x[1], rv_re * b_x[1] + rv_im * b_x[0]], axis=1)
    cn_re, cn_im = c_times_powers(1)
    m_out = jnp.stack([cn_re, -cn_im], axis=1)
    lt = jnp.stack([pw_re[:, :, t_len], pw_im[:, :, t_len]], axis=0)
    lt = jnp.transpose(lt.reshape(2, g2, SSM_PAIR * p), (1, 0, 2))
    return m_in_t, krow, m_out, lt


def _ssm(uf, b, s, ops):
    m_in_t, krow, m_out, lt = ops
    nc = s // SSM_T
    npair = PIECES // SSM_PAIR
    gl = SSM_T * SSM_GROUP
    wl = SSM_PAIR * gl
    sl2 = 2 * SSM_PAIR * SSM_STATE
    cb = _largest_divisor(nc, SSM_ROWS // b)
    rows = cb * b
    per = lambda n, *shape: pl.BlockSpec((n,) + shape, lambda bd, cr: (bd,) + (0,) * len(shape))
    io = pl.BlockSpec((npair, rows, wl), lambda bd, cr: (bd, cr, 0))
    return pl.pallas_call(
        functools.partial(_ssm_kernel, nb=b),
        grid=(uf.shape[0] // npair, nc // cb),
        in_specs=[io, per(PIECES, 2, SSM_STATE, gl), per(PIECES, SSM_GROUP, gl), per(PIECES, 2, SSM_STATE, gl),
                  per(npair, 2, sl2 // 2)],
        out_specs=io,
        out_shape=jax.ShapeDtypeStruct(uf.shape, BF16),
        scratch_shapes=[pltpu.VMEM((npair, 2, b, sl2 // 2), F32), pltpu.VMEM((npair, SSM_PAIR, gl, gl), BF16),
                        pltpu.VMEM((npair, sl2, wl), BF16), pltpu.VMEM((npair, sl2, wl), BF16),
                        pltpu.VMEM((npair, rows, sl2), F32), pltpu.VMEM((npair, rows, sl2), BF16)],
        compiler_params=pltpu.CompilerParams(
            dimension_semantics=("arbitrary", "arbitrary"), vmem_limit_bytes=VMEM_LIMIT),
        name="ssm_chunk",
    )(uf, m_in_t, krow, m_out, lt)


def _out_ffn_kernel(x_ref, att_ref, ys_ref, gate_ref, wa_ref, wg_ref, wb_ref, wo_ref,
                    gf_ref, w1_ref, w2_ref, gl_ref, o_ref, ys_s):
    nb, ts, d = x_ref.shape
    rows = nb * ts
    dot = lambda a, w: jnp.dot(a, w, preferred_element_type=F32)
    nbd = ys_s.shape[0]
    pitch = ys_s.shape[1] // nb
    gl = SSM_T * SSM_GROUP
    for bd in range(nbd):
        for jo in range(SSM_T // PIECES):
            tiles = []
            for g in range(PIECES):
                lo = (g % SSM_PAIR) * gl + jo * LANES
                tiles.append(ys_ref[bd * (PIECES // SSM_PAIR) + g // SSM_PAIR, :, lo:lo + LANES].astype(F32))
            for j8, tile in enumerate(_piece_transpose(tiles)):
                for c in range(ts // SSM_T):
                    t = c * SSM_T + jo * PIECES + j8
                    ys_s[bd, pl.ds(t, nb, stride=pitch), :] = tile[c * nb:(c + 1) * nb, :]
    ys = jnp.concatenate(
        [jnp.concatenate([ys_s[bd, bi * pitch:bi * pitch + ts, :] for bi in range(nb)], axis=0)
         for bd in range(nbd)], axis=1).astype(BF16)
    y_a = dot(att_ref[...].reshape(rows, att_ref.shape[2]), wa_ref[...])
    glu = dot(ys, wg_ref[...])
    sw = glu.shape[1] // 2
    y_b = dot((glu[:, :sw] * jax.nn.sigmoid(glu[:, sw:])).astype(BF16), wb_ref[...])
    gates = gate_ref[...].reshape(rows, 2 * d)
    mixed = gates[:, :d].astype(F32) * y_a + gates[:, d:].astype(F32) * y_b
    h = x_ref[...].reshape(rows, d) + dot(mixed.astype(BF16), wo_ref[...])
    n = _rms(h, gf_ref[...]).astype(BF16)
    dff = w1_ref.shape[1]
    fc = min(dff, 1024)
    acc = h
    for lo in range(0, dff, fc):
        f = jnp.maximum(dot(n, w1_ref[:, lo:lo + fc]), 0.0)
        acc = acc + dot((f * f).astype(BF16), w2_ref[lo:lo + fc, :])
    o_ref[...] = _rms(acc, gl_ref[...]).reshape(nb, ts, d)


def _out_ffn(x, att, ys, gates, w_a, w_glu, w_b, w_out, norm_ffn, w1, w2, norm_final, ts):
    b, s, d = x.shape
    npairs, _, pair_lanes = ys.shape
    nbd = npairs * SSM_PAIR // PIECES
    row = lambda w: pl.BlockSpec((b, ts, w), lambda i: (0, i, 0))
    weights = [w_a, w_glu, w_b, w_out]
    bf = [w.astype(BF16) for w in weights]
    w1b, w2b = w1.astype(BF16), w2.astype(BF16)
    return pl.pallas_call(
        _out_ffn_kernel,
        grid=(s // ts,),
        in_specs=[row(d), row(att.shape[2]),
                  pl.BlockSpec((npairs, (ts // SSM_T) * b, pair_lanes), lambda i: (0, i, 0)),
                  row(gates.shape[2])]
        + [_const_spec(w.shape) for w in bf]
        + [_const_spec((1, d)), _const_spec(w1b.shape), _const_spec(w2b.shape), _const_spec((1, d))],
        out_specs=row(d),
        out_shape=jax.ShapeDtypeStruct((b, s, d), F32),
        scratch_shapes=[pltpu.VMEM((nbd, b * _row_pitch(ts), LANES), F32)],
        compiler_params=pltpu.CompilerParams(
            dimension_semantics=("arbitrary",), vmem_limit_bytes=VMEM_LIMIT),
        name="out_ffn",
    )(x, att, ys, gates, *bf, norm_ffn.reshape(1, d), w1b, w2b, norm_final.reshape(1, d))


def _time_block(b, s, rows):
    ts = _largest_divisor(s // SSM_T, max(1, rows // (b * SSM_T))) * SSM_T
    assert ts % BF16_TILE_ROWS == 0, "bf16 blocks need whole packed sublane tiles"
    return ts


def kernel(x, norm_mix, w_in, b_gate, rel_bias, ssm_a_re, ssm_a_im, ssm_log_dt, ssm_b_re, ssm_b_im,
           ssm_c_re, ssm_c_im, ssm_d, w_glu, w_proj_a, w_proj_b, w_out, norm_ffn, w_ff1, w_ff2,
           norm_final):
    b, s, d = x.shape
    assert norm_mix.shape[0] == 1, "single-layer block"
    assert s % Q_BLOCK == 0 and s >= KEY_WINDOW and s % SSM_T == 0
    assert b % BF16_TILE_ROWS == 0, "batch rows fill whole bf16 sublane tiles in the SSM state scan"
    q, k, v, us, gates = _in_proj(x, norm_mix[0], w_in[0], b_gate[0], _time_block(b, s, IN_PROJ_ROWS))
    att = _band_attn(q, k, v, rel_bias[0])
    ops = _ssm_operators(ssm_a_re[0], ssm_a_im[0], ssm_log_dt[0], ssm_b_re[0], ssm_b_im[0],
                         ssm_c_re[0], ssm_c_im[0], ssm_d[0])
    ys = _ssm(us, b, s, ops)
    return _out_ffn(x, att, ys, gates, w_proj_a[0], w_glu[0], w_proj_b[0], w_out[0],
                    norm_ffn[0], w_ff1[0], w_ff2[0], norm_final, _time_block(b, s, OUT_FFN_ROWS))
```
